```python
import jax
import jax.numpy as jnp
from jax import lax
import numpy as np

D_MODEL = 2048
BATCH = 4
SEQ = 2048
DEPTH = 4
DEC_BATCH = 8
DEC_SEQ = 8
PAST_LEN = 16384
PAGE_SIZE = 128

N_A = DEPTH // 2
N_B = DEPTH - N_A
HEAD_DIM = 128
H_A = D_MODEL // HEAD_DIM
A_WIDTH = H_A * HEAD_DIM
B_WINDOWS = (128, 512, 2048)
B_DILATIONS = (1, 4, 16)
N_GROUPS = 3
H_B = D_MODEL // (2 * HEAD_DIM)
B_QK_WIDTH = N_GROUPS * H_B * HEAD_DIM
B_WIDTH = H_B * HEAD_DIM
Q_BLOCK = 128
LN_EPS = 1e-5
ALPHA = (2.0 * DEPTH) ** 0.25
BETA = (8.0 * DEPTH) ** -0.25

kernel_name = 'yoco_fox_dilated_alibi_deepnorm_step'


def _dense(key, shape, fan_in, gain=1.0):
    return jax.random.normal(key, shape, jnp.float32) * (gain * fan_in ** -0.5)


def layer_norm(x, g, b):
    xf = x.astype(jnp.float32)
    mu = xf.mean(-1, keepdims=True)
    var = jnp.square(xf - mu).mean(-1, keepdims=True)
    y = (xf - mu) * lax.rsqrt(var + LN_EPS) * g.astype(jnp.float32) + b.astype(jnp.float32)
    return y.astype(x.dtype)


def alibi_slopes():
    n = N_GROUPS * H_B
    return (2.0 ** (-8.0 * jnp.arange(1, n + 1, dtype=jnp.float32) / n)).reshape(N_GROUPS, H_B)


def fox_project(x, w_in, b_f):
    bsz, seq = x.shape[:2]
    y = x @ w_in
    q, k, v, gate, fl = jnp.split(y, [A_WIDTH, 2 * A_WIDTH, 3 * A_WIDTH, 4 * A_WIDTH], axis=-1)
    shp = (bsz, seq, H_A, HEAD_DIM)
    logf = jax.nn.log_sigmoid((fl + b_f).astype(jnp.float32))
    return q.reshape(shp), k.reshape(shp), v.reshape(shp), gate, logf


def fox_prompt(q, k, v, logf):
    bsz, seq = q.shape[:2]
    n_blk = seq // Q_BLOCK
    scale = HEAD_DIM ** -0.5
    c = lax.cumsum(logf, axis=1)
    c_key = c.transpose(0, 2, 1)[:, :, None, :]
    pos = jnp.arange(seq)
    qb = q.reshape(bsz, n_blk, Q_BLOCK, H_A, HEAD_DIM).transpose(1, 0, 2, 3, 4)
    cb = c.reshape(bsz, n_blk, Q_BLOCK, H_A).transpose(1, 0, 3, 2)
    tb = pos.reshape(n_blk, Q_BLOCK)

    def block(args):
        q_i, c_i, t_i = args
        s = jnp.einsum('bqhd,bshd->bhqs', q_i, k).astype(jnp.float32) * scale
        s = s + c_i[..., None] - c_key
        s = jnp.where(t_i[:, None] >= pos[None, :], s, -jnp.inf)
        p = jax.nn.softmax(s, axis=-1)
        return jnp.einsum('bhqs,bshd->bqhd', p.astype(v.dtype), v)

    o = lax.map(block, (qb, cb, tb))
    return o.transpose(1, 0, 2, 3, 4).reshape(bsz, seq, A_WIDTH)


def fox_sample(q, k, v, logf, past_k, past_v, past_logf):
    bsz, t_new = q.shape[:2]
    n_past = past_k.shape[1]
    scale = HEAD_DIM ** -0.5
    rc = lax.cumsum(past_logf, axis=1, reverse=True)
    suffix = jnp.concatenate([rc[:, 1:], jnp.zeros_like(rc[:, :1])], axis=1)
    cnew = lax.cumsum(logf, axis=1)
    c_q = cnew.transpose(0, 2, 1)[..., None]
    s_past = jnp.einsum('bthd,bshd->bhts', q, past_k).astype(jnp.float32) * scale
    s_past = s_past + c_q + suffix.transpose(0, 2, 1)[:, :, None, :]
    s_new = jnp.einsum('bthd,bshd->bhts', q, k).astype(jnp.float32) * scale
    s_new = s_new + c_q - cnew.transpose(0, 2, 1)[:, :, None, :]
    causal = jnp.arange(t_new)[:, None] >= jnp.arange(t_new)[None, :]
    s_new = jnp.where(causal, s_new, -jnp.inf)
    p = jax.nn.softmax(jnp.concatenate([s_past, s_new], axis=-1), axis=-1).astype(v.dtype)
    o = (jnp.einsum('bhts,bshd->bthd', p[..., :n_past], past_v)
         + jnp.einsum('bhts,bshd->bthd', p[..., n_past:], v))
    return o.reshape(bsz, t_new, A_WIDTH)


def shared_kv(x, w_kv):
    return (x @ w_kv).reshape(x.shape[0], x.shape[1], 2, N_GROUPS, H_B, HEAD_DIM)


def dilated_project(x, w_in):
    y = x @ w_in
    q = y[..., :B_QK_WIDTH].reshape(x.shape[0], x.shape[1], N_GROUPS, H_B, HEAD_DIM)
    return q, y[..., B_QK_WIDTH:]


def dilated_prompt(q, k, v, window, dilation, slopes):
    bsz, seq = q.shape[:2]
    wd = window // dilation
    scale = HEAD_DIM ** -0.5
    span = dilation * Q_BLOCK
    s_pad = -(-seq // span) * span
    sub_len = s_pad // dilation
    n_blk = sub_len // Q_BLOCK

    def split(a):
        a = jnp.pad(a, ((0, 0), (0, s_pad - seq), (0, 0), (0, 0)))
        a = a.reshape(bsz, sub_len, dilation, H_B, HEAD_DIM).transpose(0, 2, 1, 3, 4)
        return a.reshape(bsz, dilation, n_blk, Q_BLOCK, H_B, HEAD_DIM)

    def with_prev(a):
        prev = jnp.pad(a, ((0, 0), (0, 0), (1, 0), (0, 0), (0, 0), (0, 0)))[:, :, :-1]
        return jnp.concatenate([prev, a], axis=3)

    qs = split(q)
    kk, vv = with_prev(split(k)), with_prev(split(v))
    s = jnp.einsum('brnqhd,brnkhd->brnhqk', qs, kk).astype(jnp.float32) * scale
    a_idx = jnp.arange(Q_BLOCK)[:, None]
    b_idx = jnp.arange(2 * Q_BLOCK)[None, :]
    dist = Q_BLOCK + a_idx - b_idx
    blk_ids = jnp.arange(n_blk)[:, None, None]
    valid = (dist >= 0) & (dist <= wd) & ((blk_ids > 0) | (b_idx >= Q_BLOCK))
    s = s - slopes[:, None, None] * (dist * dilation).astype(jnp.float32)
    s = jnp.where(valid[:, None], s, -jnp.inf)
    m = s.max(-1, keepdims=True)
    e = jnp.exp(s - m)
    den = e.sum(-1, keepdims=True)
    o = jnp.einsum('brnhqk,brnkhd->brnqhd', (e / den).astype(v.dtype), vv)
    lse = (m + jnp.log(den))[..., 0]
    o = o.reshape(bsz, dilation, sub_len, H_B, HEAD_DIM).transpose(0, 2, 1, 3, 4)
    o = o.reshape(bsz, s_pad, H_B, HEAD_DIM)[:, :seq]
    lse = lse.transpose(0, 1, 2, 4, 3).reshape(bsz, dilation, sub_len, H_B).transpose(0, 2, 1, 3)
    lse = lse.reshape(bsz, s_pad, H_B)[:, :seq]
    return o, lse


def dilated_sample(q, cat_k, cat_v, window, dilation, slopes):
    t_new = q.shape[1]
    buf_len = cat_k.shape[1] - t_new
    wd = window // dilation
    scale = HEAD_DIM ** -0.5
    steps = jnp.arange(wd + 1)
    idx = buf_len + jnp.arange(t_new)[:, None] - steps[None, :] * dilation
    valid = idx >= 0
    idx_c = jnp.maximum(idx, 0)
    gk = cat_k[:, idx_c]
    gv = cat_v[:, idx_c]
    s = jnp.einsum('bthd,btkhd->bhtk', q, gk).astype(jnp.float32) * scale
    s = s - slopes[:, None, None] * (steps * dilation).astype(jnp.float32)
    s = jnp.where(valid, s, -jnp.inf)
    m = s.max(-1, keepdims=True)
    e = jnp.exp(s - m)
    den = e.sum(-1, keepdims=True)
    o = jnp.einsum('bhtk,btkhd->bthd', (e / den).astype(cat_v.dtype), gv)
    lse = (m + jnp.log(den))[..., 0].transpose(0, 2, 1)
    return o, lse


def merge_groups(results):
    o = jnp.stack([r[0] for r in results])
    w = jax.nn.softmax(jnp.stack([r[1] for r in results]), axis=0)
    y = jnp.einsum('gbsh,gbshd->bshd', w.astype(o.dtype), o)
    return y.reshape(y.shape[0], y.shape[1], B_WIDTH)


def setup_inputs(seed: int = 0) -> dict:
    key = jax.random.key(seed)
    ks = jax.random.split(key, 24)
    n_pages = PAST_LEN // PAGE_SIZE
    n_pool = (5 * DEC_BATCH * n_pages + 3) // 4
    x_prompt = jax.random.normal(ks[0], (BATCH, SEQ, D_MODEL), jnp.float32)
    x_sample = jax.random.normal(ks[1], (DEC_BATCH, DEC_SEQ, D_MODEL), jnp.float32)
    cache_a_kv = jax.random.normal(ks[2], (N_A, n_pool, PAGE_SIZE, 2, H_A, HEAD_DIM), jnp.float32)
    cache_a_logf = jax.nn.log_sigmoid(9.0 + 0.5 * jax.random.normal(ks[3], (N_A, n_pool, PAGE_SIZE, H_A), jnp.float32))
    cache_b0_kv = jax.random.normal(ks[4], (DEC_BATCH, min(B_WINDOWS[0], PAST_LEN), 2, H_B, HEAD_DIM), jnp.float32)
    cache_b1_kv = jax.random.normal(ks[5], (DEC_BATCH, min(B_WINDOWS[1], PAST_LEN), 2, H_B, HEAD_DIM), jnp.float32)
    cache_b2_kv = jax.random.normal(ks[6], (DEC_BATCH, min(B_WINDOWS[2], PAST_LEN), 2, H_B, HEAD_DIM), jnp.float32)
    page_table = jax.random.permutation(ks[7], n_pool)[:DEC_BATCH * n_pages].reshape(DEC_BATCH, n_pages).astype(jnp.int32)
    w_in_a = jnp.concatenate([
        _dense(ks[8], (N_A, D_MODEL, 2 * A_WIDTH), D_MODEL),
        _dense(ks[9], (N_A, D_MODEL, A_WIDTH), D_MODEL, BETA),
        _dense(ks[10], (N_A, D_MODEL, A_WIDTH), D_MODEL),
        _dense(ks[11], (N_A, D_MODEL, H_A), D_MODEL),
    ], axis=-1)
    b_f_a = jax.random.uniform(ks[12], (N_A, H_A), jnp.float32, 1.0, 6.0)
    w_o_a = _dense(ks[13], (N_A, A_WIDTH, D_MODEL), A_WIDTH, BETA)
    w_kv_b = jnp.concatenate([
        _dense(ks[14], (D_MODEL, B_QK_WIDTH), D_MODEL),
        _dense(ks[15], (D_MODEL, B_QK_WIDTH), D_MODEL, BETA),
    ], axis=-1)
    w_in_b = _dense(ks[16], (N_B, D_MODEL, B_QK_WIDTH + B_WIDTH), D_MODEL)
    w_o_b = _dense(ks[17], (N_B, B_WIDTH, D_MODEL), B_WIDTH, BETA)
    ln_g = 1.0 + 0.02 * jax.random.normal(ks[18], (DEPTH, D_MODEL), jnp.float32)
    ln_b = 0.02 * jax.random.normal(ks[19], (DEPTH, D_MODEL), jnp.float32)
    return {'x_prompt': x_prompt, 'x_sample': x_sample, 'cache_a_kv': cache_a_kv,
            'cache_a_logf': cache_a_logf, 'cache_b0_kv': cache_b0_kv, 'cache_b1_kv': cache_b1_kv,
            'cache_b2_kv': cache_b2_kv, 'page_table': page_table, 'w_in_a': w_in_a, 'b_f_a': b_f_a,
            'w_o_a': w_o_a, 'w_kv_b': w_kv_b, 'w_in_b': w_in_b, 'w_o_b': w_o_b,
            'ln_g': ln_g, 'ln_b': ln_b}


def reference(x_prompt, x_sample, cache_a_kv, cache_a_logf, cache_b0_kv, cache_b1_kv, cache_b2_kv,
              page_table, w_in_a, b_f_a, w_o_a, w_kv_b, w_in_b, w_o_b, ln_g, ln_b):
    slopes = alibi_slopes()
    bsz_s = x_sample.shape[0]
    seq_p = x_prompt.shape[1]
    n_past = page_table.shape[1] * PAGE_SIZE
    b_bufs = (cache_b0_kv, cache_b1_kv, cache_b2_kv)
    xp, xs = x_prompt, x_sample
    kv_a_p, logf_a_p, kv_a_s, logf_a_s = [], [], [], []
    kv_p, cat_s = None, None
    for layer in range(DEPTH):
        if layer < N_A:
            q, k, v, gate, logf = fox_project(xp, w_in_a[layer], b_f_a[layer])
            mix_p = (fox_prompt(q, k, v, logf) * jax.nn.silu(gate)) @ w_o_a[layer]
            kv_a_p.append(jnp.stack([k, v], axis=2))
            logf_a_p.append(logf)
            q, k, v, gate, logf = fox_project(xs, w_in_a[layer], b_f_a[layer])
            past_kv = cache_a_kv[layer][page_table].reshape(bsz_s, n_past, 2, H_A, HEAD_DIM)
            past_logf = cache_a_logf[layer][page_table].reshape(bsz_s, n_past, H_A).astype(jnp.float32)
            o = fox_sample(q, k, v, logf, past_kv[:, :, 0], past_kv[:, :, 1], past_logf)
            mix_s = (o * jax.nn.silu(gate)) @ w_o_a[layer]
            kv_a_s.append(jnp.stack([k, v], axis=2))
            logf_a_s.append(logf)
        else:
            if layer == N_A:
                kv_p = shared_kv(xp, w_kv_b)
                kv_s = shared_kv(xs, w_kv_b)
                cat_s = [jnp.concatenate([b_bufs[g], kv_s[:, :, :, g]], axis=1) for g in range(N_GROUPS)]
            j = layer - N_A
            q, gate = dilated_project(xp, w_in_b[j])
            res = [dilated_prompt(q[:, :, g], kv_p[:, :, 0, g], kv_p[:, :, 1, g],
                                  B_WINDOWS[g], B_DILATIONS[g], slopes[g]) for g in range(N_GROUPS)]
            mix_p = (merge_groups(res) * jax.nn.silu(gate)) @ w_o_b[j]
            q, gate = dilated_project(xs, w_in_b[j])
            res = [dilated_sample(q[:, :, g], cat_s[g][:, :, 0], cat_s[g][:, :, 1],
                                  B_WINDOWS[g], B_DILATIONS[g], slopes[g]) for g in range(N_GROUPS)]
            mix_s = (merge_groups(res) * jax.nn.silu(gate)) @ w_o_b[j]
        xp = layer_norm(ALPHA * xp + mix_p, ln_g[layer], ln_b[layer])
        xs = layer_norm(ALPHA * xs + mix_s, ln_g[layer], ln_b[layer])
    new_a_kv_prompt = jnp.stack(kv_a_p)
    new_a_logf_prompt = jnp.stack(logf_a_p)
    new_a_kv_sample = jnp.stack(kv_a_s)
    new_a_logf_sample = jnp.stack(logf_a_s)
    new_b0_kv_prompt = kv_p[:, seq_p - min(B_WINDOWS[0], seq_p):, :, 0]
    new_b1_kv_prompt = kv_p[:, seq_p - min(B_WINDOWS[1], seq_p):, :, 1]
    new_b2_kv_prompt = kv_p[:, seq_p - min(B_WINDOWS[2], seq_p):, :, 2]
    c0, c1, c2 = cat_s[0].shape[1], cat_s[1].shape[1], cat_s[2].shape[1]
    new_b0_kv_sample = cat_s[0][:, c0 - min(B_WINDOWS[0], c0):]
    new_b1_kv_sample = cat_s[1][:, c1 - min(B_WINDOWS[1], c1):]
    new_b2_kv_sample = cat_s[2][:, c2 - min(B_WINDOWS[2], c2):]
    return (xp, xs, new_a_kv_prompt, new_a_logf_prompt, new_a_kv_sample, new_a_logf_sample,
            new_b0_kv_prompt, new_b1_kv_prompt, new_b2_kv_prompt,
            new_b0_kv_sample, new_b1_kv_sample, new_b2_kv_sample)
```

```python
import functools

import jax
import jax.numpy as jnp
from jax import lax
from jax.experimental import pallas as pl
from jax.experimental.pallas import tpu as pltpu

F32 = jnp.float32
BF16 = jnp.bfloat16

HEAD_DIM = 128
PAGE_SIZE = 128
B_WINDOWS = (128, 512, 2048)
B_DILATIONS = (1, 4, 16)
N_GROUPS = 3
Q_BLOCK = 128
LN_EPS = 1e-5
NEG = -1e30
V7X_VMEM_LIMIT_BYTES = 56 * 1024 * 1024


def _params(*sem):
    return pltpu.CompilerParams(dimension_semantics=sem, vmem_limit_bytes=V7X_VMEM_LIMIT_BYTES)


def _split3(x):
    hi = x.astype(BF16)
    r1 = x - hi.astype(F32)
    mid = r1.astype(BF16)
    lo = (r1 - mid.astype(F32)).astype(BF16)
    return hi, mid, lo


def _dot(a, b):
    return jnp.dot(a, b, preferred_element_type=F32)


def _dot_nt(a, b):
    return lax.dot_general(a, b, (((1,), (1,)), ((), ())), preferred_element_type=F32)


def _div_mod_pow2(x, n):
    assert n & (n - 1) == 0
    return x >> (n.bit_length() - 1), x & (n - 1)


def _silu(x):
    return x * (1.0 / (1.0 + jnp.exp(-x)))


def _mm_kernel(x_ref, w_ref, *o_refs, mode, scale):
    acc = _dot(x_ref[...].astype(BF16), w_ref[...])
    if mode == "scaled_bf16":
        o_refs[0][...] = (acc * scale).astype(BF16)
    elif mode == "f32_and_bf16":
        o_refs[0][...] = acc
        o_refs[1][...] = acc.astype(BF16)
    elif mode == "silu":
        o_refs[0][...] = _silu(acc)
    else:
        o_refs[0][...] = acc


def _mm(x, w, layer, col0, n, mode, scale=1.0):
    m, k = x.shape
    tn = min(n, 1024)
    tm = min(m, 1024 if x.dtype == BF16 else 512)
    assert m % tm == 0 and n % tn == 0 and col0 % tn == 0
    cb = col0 // tn
    out_shape = [jax.ShapeDtypeStruct((m, n), BF16 if mode == "scaled_bf16" else F32)]
    if mode == "f32_and_bf16":
        out_shape.append(jax.ShapeDtypeStruct((m, n), BF16))
    out_specs = [pl.BlockSpec((tm, tn), lambda i, j: (i, j)) for _ in out_shape]
    res = pl.pallas_call(
        functools.partial(_mm_kernel, mode=mode, scale=scale),
        grid=(m // tm, n // tn),
        in_specs=[pl.BlockSpec((tm, k), lambda i, j: (i, 0)),
                  pl.BlockSpec((None, k, tn), lambda i, j: (layer, 0, cb + j))],
        out_specs=out_specs,
        out_shape=out_shape,
        compiler_params=_params("parallel", "parallel"),
        name="proj_" + mode,
    )(x, w)
    return res if mode == "f32_and_bf16" else res[0]


def _out_ln_kernel(a_ref, w_ref, x_ref, g_ref, b_ref, y_ref, ybf_ref, *, alpha):
    mix = _dot(a_ref[...].astype(BF16), w_ref[...])
    z = alpha * x_ref[...] + mix
    mu = jnp.mean(z, axis=-1, keepdims=True)
    zc = z - mu
    var = jnp.mean(zc * zc, axis=-1, keepdims=True)
    y = zc * lax.rsqrt(var + LN_EPS) * g_ref[...] + b_ref[...]
    y_ref[...] = y
    ybf_ref[...] = y.astype(BF16)


def _out_ln(a, w, layer, x, g, b, alpha):
    m, k = a.shape
    d = x.shape[1]
    tm = min(m, 512)
    return pl.pallas_call(
        functools.partial(_out_ln_kernel, alpha=alpha),
        grid=(m // tm,),
        in_specs=[pl.BlockSpec((tm, k), lambda i: (i, 0)),
                  pl.BlockSpec((None, k, d), lambda i: (layer, 0, 0)),
                  pl.BlockSpec((tm, d), lambda i: (i, 0)),
                  pl.BlockSpec((1, d), lambda i: (0, 0)),
                  pl.BlockSpec((1, d), lambda i: (0, 0))],
        out_specs=[pl.BlockSpec((tm, d), lambda i: (i, 0)),
                   pl.BlockSpec((tm, d), lambda i: (i, 0))],
        out_shape=[jax.ShapeDtypeStruct((m, d), F32), jax.ShapeDtypeStruct((m, d), BF16)],
        compiler_params=_params("parallel"),
        name="out_ln",
    )(a, w, x, g, b)


def _log_sigmoid(z):
    return jnp.minimum(z, 0.0) - jnp.log1p(jnp.exp(-jnp.abs(z)))


def _logf_cumsum_kernel(fl_ref, bf_ref, logf_ref, c_ref, ct_ref, *, n_heads, chunk):
    seq = fl_ref.shape[1]
    logf = _log_sigmoid(fl_ref[0] + bf_ref[...])
    logf_ref[0] = logf[:, :n_heads]
    r = lax.broadcasted_iota(jnp.int32, (chunk, chunk), 0)
    c = lax.broadcasted_iota(jnp.int32, (chunk, chunk), 1)
    tri = jnp.where(r >= c, 1.0, 0.0).astype(BF16)
    carry = jnp.zeros((1, logf.shape[1]), F32)
    parts = []
    for i in range(seq // chunk):
        hi, mid, lo = _split3(logf[i * chunk:(i + 1) * chunk])
        cs = (_dot(tri, hi) + _dot(tri, mid)) + _dot(tri, lo) + carry
        carry = cs[chunk - 1:chunk]
        parts.append(cs)
    cum = jnp.concatenate(parts, axis=0)
    c_ref[0] = cum[:, :n_heads]
    ct_ref[0] = cum.T[:n_heads, :]


def _logf_cumsum(fl, bf_pad, bsz, seq, n_heads):
    lanes = fl.shape[-1]
    return pl.pallas_call(
        functools.partial(_logf_cumsum_kernel, n_heads=n_heads, chunk=256),
        grid=(bsz,),
        in_specs=[pl.BlockSpec((1, seq, lanes), lambda b: (b, 0, 0)),
                  pl.BlockSpec((1, lanes), lambda b: (0, 0))],
        out_specs=[pl.BlockSpec((1, seq, n_heads), lambda b: (b, 0, 0)),
                   pl.BlockSpec((1, seq, n_heads), lambda b: (b, 0, 0)),
                   pl.BlockSpec((1, n_heads, seq), lambda b: (b, 0, 0))],
        out_shape=[jax.ShapeDtypeStruct((bsz, seq, n_heads), F32),
                   jax.ShapeDtypeStruct((bsz, seq, n_heads), F32),
                   jax.ShapeDtypeStruct((bsz, n_heads, seq), F32)],
        compiler_params=_params("parallel"),
        name="logf_cumsum",
    )(fl.reshape(bsz, seq, lanes), bf_pad)


def _logf_kernel(fl_ref, bf_ref, logf_ref, *, n_heads):
    logf_ref[...] = _log_sigmoid(fl_ref[...] + bf_ref[...])[:, :n_heads]


def _logf(fl, bf_pad, n_heads):
    m, lanes = fl.shape
    return pl.pallas_call(
        functools.partial(_logf_kernel, n_heads=n_heads),
        grid=(1,),
        in_specs=[pl.BlockSpec((m, lanes), lambda i: (0, 0)),
                  pl.BlockSpec((1, lanes), lambda i: (0, 0))],
        out_specs=pl.BlockSpec((m, n_heads), lambda i: (0, 0)),
        out_shape=jax.ShapeDtypeStruct((m, n_heads), F32),
        name="logf",
    )(fl, bf_pad)


def _fox_prompt_kernel(q_ref, k_ref, v_ref, g_ref, cq_ref, ck_ref, o_ref, *, t, n_heads):
    qi = pl.program_id(1)
    rows = lax.broadcasted_iota(jnp.int32, (t, t), 0)
    cols = lax.broadcasted_iota(jnp.int32, (t, t), 1)
    causal = rows >= cols
    for h in range(n_heads):
        sl = slice(h * HEAD_DIM, (h + 1) * HEAD_DIM)
        q_h = q_ref[0, :, sl]
        cq_h = cq_ref[0, :, h:h + 1]

        def step(kj, carry, masked, sl=sl, h=h, q_h=q_h, cq_h=cq_h):
            m, l, acc = carry
            r0 = pl.multiple_of(kj * t, t)
            k_h = k_ref[0, pl.ds(r0, t), sl]
            v_h = v_ref[0, pl.ds(r0, t), sl]
            ck_h = ck_ref[0, h, pl.ds(kj, 1), :]
            s = _dot_nt(q_h, k_h) + (cq_h - ck_h)
            if masked:
                s = jnp.where(causal, s, NEG)
            m_new = jnp.maximum(m, jnp.max(s, axis=-1, keepdims=True))
            alpha = jnp.exp(m - m_new)
            p = jnp.exp(s - m_new)
            l_new = alpha * l + jnp.sum(p, axis=-1, keepdims=True)
            acc_new = alpha * acc + _dot(p.astype(BF16), v_h)
            return m_new, l_new, acc_new

        init = (jnp.full((t, 1), NEG, F32), jnp.zeros((t, 1), F32), jnp.zeros((t, HEAD_DIM), F32))
        carry = lax.fori_loop(0, qi, functools.partial(step, masked=False), init)
        _, l, acc = step(qi, carry, True)
        o_ref[0, :, sl] = (acc * (1.0 / l) * g_ref[0, :, sl]).astype(o_ref.dtype)


def _fox_prompt(q, kv_bf, gate, c, ct, bsz, seq, n_heads):
    a_width = n_heads * HEAD_DIM
    t = 256
    nblk = seq // t
    out = pl.pallas_call(
        functools.partial(_fox_prompt_kernel, t=t, n_heads=n_heads),
        grid=(bsz, nblk),
        in_specs=[pl.BlockSpec((1, t, a_width), lambda b, i: (b, i, 0)),
                  pl.BlockSpec((1, seq, a_width), lambda b, i: (b, 0, 0)),
                  pl.BlockSpec((1, seq, a_width), lambda b, i: (b, 0, 1)),
                  pl.BlockSpec((1, t, a_width), lambda b, i: (b, i, 0)),
                  pl.BlockSpec((1, t, n_heads), lambda b, i: (b, i, 0)),
                  pl.BlockSpec((1, n_heads, nblk, t), lambda b, i: (b, 0, 0, 0))],
        out_specs=pl.BlockSpec((1, t, a_width), lambda b, i: (b, i, 0)),
        out_shape=jax.ShapeDtypeStruct((bsz, seq, a_width), BF16),
        compiler_params=_params("parallel", "arbitrary"),
        name="fox_prompt",
    )(q.reshape(bsz, seq, a_width), kv_bf.reshape(bsz, seq, 2 * a_width),
      kv_bf.reshape(bsz, seq, 2 * a_width), gate.reshape(bsz, seq, a_width),
      c, ct.reshape(bsz, n_heads, nblk, t))
    return out.reshape(bsz * seq, a_width)


def _block_diag_queries(qr, qbd_sc, n_heads, n_tok):
    rows = qr.shape[0]
    head_of_row, _ = _div_mod_pow2(lax.broadcasted_iota(jnp.int32, (rows, HEAD_DIM), 0), n_tok)
    for h in range(n_heads):
        qbd_sc[:, h * HEAD_DIM:(h + 1) * HEAD_DIM] = jnp.where(head_of_row == h, qr, jnp.zeros_like(qr))


def _softmax_first(s, v, m_sc, l_sc, acc_sc):
    m = jnp.max(s, axis=-1, keepdims=True)
    p = jnp.exp(s - m)
    m_sc[...] = m
    l_sc[...] = jnp.sum(p, axis=-1, keepdims=True)
    acc_sc[...] = _dot(p.astype(BF16), v)


def _softmax_next(s, v, m_sc, l_sc, acc_sc):
    m_old = m_sc[...]
    m_new = jnp.maximum(m_old, jnp.max(s, axis=-1, keepdims=True))
    alpha = jnp.exp(m_old - m_new)
    p = jnp.exp(s - m_new)
    m_sc[...] = m_new
    l_sc[...] = alpha * l_sc[...] + jnp.sum(p, axis=-1, keepdims=True)
    acc_sc[...] = alpha * acc_sc[...] + _dot(p.astype(BF16), v)


def _fox_sample_kernel(pt_ref, qr_ref, lf_ref, new_ref, g_ref, page_ref, lt_ref, o_ref,
                       qbd_sc, m_sc, l_sc, acc_sc, carry_sc, cq_sc, *, n_heads, n_tok):
    del pt_ref
    s_idx = pl.program_id(1)
    a_width = n_heads * HEAD_DIM
    rows = n_heads * n_tok
    row = lax.broadcasted_iota(jnp.int32, (rows, PAGE_SIZE), 0)
    lane = lax.broadcasted_iota(jnp.int32, (rows, PAGE_SIZE), 1)
    h_r, t_r = _div_mod_pow2(row, n_tok)

    @pl.when(s_idx == 0)
    def _():
        _block_diag_queries(qr_ref[0], qbd_sc, n_heads, n_tok)
        lf = lf_ref[0]
        t_l, h_l = _div_mod_pow2(lane, n_heads)
        same_head = h_l == h_r
        cnew = jnp.sum(jnp.where(same_head & (t_l <= t_r), lf, 0.0), axis=-1, keepdims=True)
        bias = jnp.zeros((rows, PAGE_SIZE), F32)
        for s in range(n_tok):
            c_s = jnp.sum(jnp.where(same_head & (t_l <= s), lf, 0.0), axis=-1, keepdims=True)
            bias = jnp.where(lane == s, cnew - c_s, bias)
        kv = new_ref[0]
        k = kv[:, :a_width].astype(BF16)
        v = kv[:, a_width:].astype(BF16)
        s_new = jnp.where(lane <= t_r, _dot_nt(qbd_sc[...], k) + bias, NEG)
        _softmax_first(s_new, v, m_sc, l_sc, acc_sc)
        carry_sc[...] = jnp.zeros_like(carry_sc)
        cq_sc[...] = cnew

    @pl.when(s_idx > 0)
    def _():
        lt = lt_ref[0]
        lt_rep = jnp.broadcast_to(lt[:, None, :], (n_heads, n_tok, PAGE_SIZE)).reshape(rows, PAGE_SIZE)
        later = (lax.broadcasted_iota(jnp.int32, (PAGE_SIZE, PAGE_SIZE), 0)
                 > lax.broadcasted_iota(jnp.int32, (PAGE_SIZE, PAGE_SIZE), 1))
        later = jnp.where(later, 1.0, 0.0).astype(BF16)
        hi, mid, lo = _split3(lt_rep)
        suffix = (_dot(hi, later) + _dot(mid, later)) + _dot(lo, later)
        bias = suffix + (carry_sc[...] + cq_sc[...])
        carry_sc[...] = carry_sc[...] + jnp.sum(lt_rep, axis=-1, keepdims=True)
        kv = page_ref[0]
        k = kv[:, :a_width].astype(BF16)
        v = kv[:, a_width:].astype(BF16)
        _softmax_next(_dot_nt(qbd_sc[...], k) + bias, v, m_sc, l_sc, acc_sc)

    @pl.when(s_idx == pl.num_programs(1) - 1)
    def _():
        inv_l = 1.0 / l_sc[...]
        for h in range(n_heads):
            sl = slice(h * HEAD_DIM, (h + 1) * HEAD_DIM)
            rs = slice(h * n_tok, (h + 1) * n_tok)
            o_ref[0, :, sl] = acc_sc[rs, sl] * inv_l[rs] * g_ref[0, :, sl]


def _fox_sample(q_bf, kv_new, logf_new, gate, cache_kv, cache_logf, page_table, layer, n_heads):
    bs, n_pages = page_table.shape
    n_tok = q_bf.shape[0] // bs
    a_width = n_heads * HEAD_DIM
    rows = n_heads * n_tok
    assert rows == PAGE_SIZE and n_tok * n_heads == PAGE_SIZE
    n_pool = cache_kv.shape[1]
    qr = q_bf.reshape(bs, n_tok, n_heads, HEAD_DIM).transpose(0, 2, 1, 3).reshape(bs, rows, HEAD_DIM)
    lf = logf_new.reshape(bs, 1, n_tok * n_heads)
    new_pad = jnp.pad(kv_new.reshape(bs, n_tok, 2 * a_width), ((0, 0), (0, PAGE_SIZE - n_tok), (0, 0)))
    pages = cache_kv.reshape(cache_kv.shape[0], n_pool, PAGE_SIZE, 2 * a_width)
    logf_t = cache_logf[layer].transpose(0, 2, 1)

    def page_idx(b, s, pt):
        return (layer, pt[b, n_pages - jnp.maximum(s, 1)], 0, 0)

    def logf_idx(b, s, pt):
        return (pt[b, n_pages - jnp.maximum(s, 1)], 0, 0)

    per_seq = lambda b, s, pt: (b, 0, 0)
    out = pl.pallas_call(
        functools.partial(_fox_sample_kernel, n_heads=n_heads, n_tok=n_tok),
        grid_spec=pltpu.PrefetchScalarGridSpec(
            num_scalar_prefetch=1,
            grid=(bs, n_pages + 1),
            in_specs=[pl.BlockSpec((1, rows, HEAD_DIM), per_seq),
                      pl.BlockSpec((1, 1, n_tok * n_heads), per_seq),
                      pl.BlockSpec((1, PAGE_SIZE, 2 * a_width), per_seq),
                      pl.BlockSpec((1, n_tok, a_width), per_seq),
                      pl.BlockSpec((None, 1, PAGE_SIZE, 2 * a_width), page_idx),
                      pl.BlockSpec((1, n_heads, PAGE_SIZE), logf_idx)],
            out_specs=pl.BlockSpec((1, n_tok, a_width), per_seq),
            scratch_shapes=[pltpu.VMEM((rows, a_width), BF16),
                            pltpu.VMEM((rows, 1), F32),
                            pltpu.VMEM((rows, 1), F32),
                            pltpu.VMEM((rows, a_width), F32),
                            pltpu.VMEM((rows, 1), F32),
                            pltpu.VMEM((rows, 1), F32)]),
        out_shape=jax.ShapeDtypeStruct((bs, n_tok, a_width), F32),
        compiler_params=_params("parallel", "arbitrary"),
        name="fox_sample",
    )(page_table, qr, lf, new_pad, gate.reshape(bs, n_tok, a_width), pages, logf_t)
    return out.reshape(bs * n_tok, a_width)


def _alibi_slopes(group, n_heads):
    n = N_GROUPS * n_heads
    return tuple(2.0 ** (-8.0 * (group * n_heads + h + 1) / n) for h in range(n_heads))


def _dil_prompt_kernel(q_ref, kc_ref, kp_ref, vc_ref, vp_ref, o_ref, lse_ref, *, dil, slopes):
    n = pl.program_id(2)
    a = lax.broadcasted_iota(jnp.int32, (Q_BLOCK, Q_BLOCK), 0)
    b = lax.broadcasted_iota(jnp.int32, (Q_BLOCK, Q_BLOCK), 1)
    valid_c = a >= b
    valid_p = b >= a
    dist_c = ((a - b) * dil).astype(F32)
    dist_p = ((Q_BLOCK + a - b) * dil).astype(F32)
    first_block = jnp.where(n > 0, 0.0, NEG)
    for h, slope in enumerate(slopes):
        sl = slice(h * HEAD_DIM, (h + 1) * HEAD_DIM)
        q_h = q_ref[0, :, sl]
        s_c = jnp.where(valid_c, _dot_nt(q_h, kc_ref[0, :, sl]) - slope * dist_c, NEG)
        s_p = jnp.where(valid_p, _dot_nt(q_h, kp_ref[0, :, sl]) - slope * dist_p + first_block, NEG)
        m = jnp.maximum(jnp.max(s_c, axis=-1, keepdims=True), jnp.max(s_p, axis=-1, keepdims=True))
        e_c = jnp.exp(s_c - m)
        e_p = jnp.exp(s_p - m)
        den = jnp.sum(e_c, axis=-1, keepdims=True) + jnp.sum(e_p, axis=-1, keepdims=True)
        o = _dot(e_c.astype(BF16), vc_ref[0, :, sl]) + _dot(e_p.astype(BF16), vp_ref[0, :, sl])
        o_ref[0, :, sl] = o * (1.0 / den)
        lse_ref[0, 0, :, h:h + 1] = m + jnp.log(den)


def _dil_prompt(q_bf, kv_bf, group, bsz, seq, n_heads):
    dil = B_DILATIONS[group]
    width = n_heads * HEAD_DIM
    sub = seq // dil
    assert sub % Q_BLOCK == 0 and B_WINDOWS[group] // dil == Q_BLOCK
    nblk = sub // Q_BLOCK
    n_q_groups = q_bf.shape[1] // width
    q_v = q_bf.reshape(bsz, sub, dil * n_q_groups * width)
    kv_v = kv_bf.reshape(bsz, sub, dil * 2 * width)
    cur = lambda b, r, n: (b, n, 0)
    q_idx = lambda b, r, n: (b, n, r * n_q_groups + group)
    kc = lambda b, r, n: (b, n, 2 * r)
    kp = lambda b, r, n: (b, jnp.maximum(n - 1, 0), 2 * r)
    vc = lambda b, r, n: (b, n, 2 * r + 1)
    vp = lambda b, r, n: (b, jnp.maximum(n - 1, 0), 2 * r + 1)
    del cur
    blk = (1, Q_BLOCK, width)
    o, lse = pl.pallas_call(
        functools.partial(_dil_prompt_kernel, dil=dil, slopes=_alibi_slopes(group, n_heads)),
        grid=(bsz, dil, nblk),
        in_specs=[pl.BlockSpec(blk, q_idx), pl.BlockSpec(blk, kc), pl.BlockSpec(blk, kp),
                  pl.BlockSpec(blk, vc), pl.BlockSpec(blk, vp)],
        out_specs=[pl.BlockSpec(blk, lambda b, r, n: (b, n, r)),
                   pl.BlockSpec((1, 1, Q_BLOCK, n_heads), lambda b, r, n: (b, r, n, 0))],
        out_shape=[jax.ShapeDtypeStruct((bsz, sub, dil * width), F32),
                   jax.ShapeDtypeStruct((bsz, dil, sub, n_heads), F32)],
        compiler_params=_params("parallel", "parallel", "arbitrary"),
        name=f"dil_prompt_g{group}",
    )(q_v, kv_v, kv_v, kv_v, kv_v)
    lse = lse.transpose(0, 2, 1, 3).reshape(bsz * seq, n_heads)
    return o.reshape(bsz * seq, width), lse


def _dil_sample_kernel(qr_ref, new_ref, buf_ref, o_ref, lse_ref, qbd_sc, m_sc, l_sc, acc_sc,
                       *, window, dil, chunk, slopes, n_tok):
    s_idx = pl.program_id(1)
    n_heads = len(slopes)
    width = n_heads * HEAD_DIM
    rows = n_heads * n_tok

    def geometry(lanes):
        row = lax.broadcasted_iota(jnp.int32, (rows, lanes), 0)
        lane = lax.broadcasted_iota(jnp.int32, (rows, lanes), 1)
        h_r, t_r = _div_mod_pow2(row, n_tok)
        slope = jnp.zeros((rows, lanes), F32)
        for h, sv in enumerate(slopes):
            slope = jnp.where(h_r == h, sv, slope)
        return t_r, lane, slope

    def scores(kv, delta, valid, slope):
        k = kv[:, :width].astype(BF16)
        v = kv[:, width:].astype(BF16)
        s = _dot_nt(qbd_sc[...], k) - slope * delta.astype(F32)
        return jnp.where(valid, s, NEG), v

    @pl.when(s_idx == 0)
    def _():
        _block_diag_queries(qr_ref[0], qbd_sc, n_heads, n_tok)
        t_r, lane, slope = geometry(new_ref.shape[1])
        delta = t_r - lane
        valid = (delta >= 0) & ((delta & (dil - 1)) == 0)
        s, v = scores(new_ref[0], delta, valid, slope)
        _softmax_first(s, v, m_sc, l_sc, acc_sc)

    @pl.when(s_idx > 0)
    def _():
        t_r, lane, slope = geometry(chunk)
        delta = window + t_r - ((s_idx - 1) * chunk + lane)
        valid = ((delta & (dil - 1)) == 0) & (delta <= window)
        s, v = scores(buf_ref[0], delta, valid, slope)
        _softmax_next(s, v, m_sc, l_sc, acc_sc)

    @pl.when(s_idx == pl.num_programs(1) - 1)
    def _():
        l = l_sc[...]
        inv_l = 1.0 / l
        for h in range(n_heads):
            sl = slice(h * HEAD_DIM, (h + 1) * HEAD_DIM)
            rs = slice(h * n_tok, (h + 1) * n_tok)
            o_ref[0, :, sl] = acc_sc[rs, sl] * inv_l[rs]
        lse_ref[0] = m_sc[...] + jnp.log(l)


def _dil_sample(q_bf, kv_new, buf, group, bs, n_tok, n_heads):
    window, dil = B_WINDOWS[group], B_DILATIONS[group]
    assert dil & (dil - 1) == 0
    width = n_heads * HEAD_DIM
    rows = n_heads * n_tok
    buf_len = buf.shape[1]
    assert buf_len == window, "the sample path assumes a full window buffer"
    chunk = min(buf_len, 512)
    new_rows = 128
    qg = q_bf[:, group * width:(group + 1) * width]
    qr = qg.reshape(bs, n_tok, n_heads, HEAD_DIM).transpose(0, 2, 1, 3).reshape(bs, rows, HEAD_DIM)
    new_pad = jnp.pad(kv_new.reshape(bs, n_tok, 2 * width), ((0, 0), (0, new_rows - n_tok), (0, 0)))
    per_seq = lambda b, s: (b, 0, 0)
    o, lse = pl.pallas_call(
        functools.partial(_dil_sample_kernel, window=window, dil=dil, chunk=chunk,
                          slopes=_alibi_slopes(group, n_heads), n_tok=n_tok),
        grid=(bs, buf_len // chunk + 1),
        in_specs=[pl.BlockSpec((1, rows, HEAD_DIM), per_seq),
                  pl.BlockSpec((1, new_rows, 2 * width), per_seq),
                  pl.BlockSpec((1, chunk, 2 * width), lambda b, s: (b, jnp.maximum(s - 1, 0), 0))],
        out_specs=[pl.BlockSpec((1, n_tok, width), per_seq),
                   pl.BlockSpec((1, rows, 1), per_seq)],
        out_shape=[jax.ShapeDtypeStruct((bs, n_tok, width), F32),
                   jax.ShapeDtypeStruct((bs, rows, 1), F32)],
        scratch_shapes=[pltpu.VMEM((rows, width), BF16),
                        pltpu.VMEM((rows, 1), F32),
                        pltpu.VMEM((rows, 1), F32),
                        pltpu.VMEM((rows, width), F32)],
        compiler_params=_params("parallel", "arbitrary"),
        name=f"dil_sample_g{group}",
    )(qr, new_pad, buf.reshape(bs, buf_len, 2 * width))
    lse = lse.reshape(bs, n_heads, n_tok).transpose(0, 2, 1).reshape(bs * n_tok, n_heads)
    return o.reshape(bs * n_tok, width), lse


def _merge_kernel(o0_ref, o1_ref, o2_ref, l0_ref, l1_ref, l2_ref, g_ref, a_ref, *, n_heads):
    l0, l1, l2 = l0_ref[...], l1_ref[...], l2_ref[...]
    mx = jnp.maximum(jnp.maximum(l0, l1), l2)
    e0, e1, e2 = jnp.exp(l0 - mx), jnp.exp(l1 - mx), jnp.exp(l2 - mx)
    inv = 1.0 / (e0 + e1 + e2)
    w0, w1, w2 = e0 * inv, e1 * inv, e2 * inv
    for h in range(n_heads):
        sl = slice(h * HEAD_DIM, (h + 1) * HEAD_DIM)
        hs = slice(h, h + 1)
        y = w0[:, hs] * o0_ref[:, sl] + w1[:, hs] * o1_ref[:, sl] + w2[:, hs] * o2_ref[:, sl]
        a_ref[:, sl] = (y * g_ref[:, sl]).astype(a_ref.dtype)


def _merge(outs, lses, gate, n_heads):
    m, width = gate.shape
    tm = min(m, 512)
    big = pl.BlockSpec((tm, width), lambda i: (i, 0))
    small = pl.BlockSpec((tm, n_heads), lambda i: (i, 0))
    return pl.pallas_call(
        functools.partial(_merge_kernel, n_heads=n_heads),
        grid=(m // tm,),
        in_specs=[big, big, big, small, small, small, big],
        out_specs=big,
        out_shape=jax.ShapeDtypeStruct((m, width), BF16),
        compiler_params=_params("parallel"),
        name="merge_groups",
    )(*outs, *lses, gate)


def kernel(x_prompt, x_sample, cache_a_kv, cache_a_logf, cache_b0_kv, cache_b1_kv, cache_b2_kv, page_table,
           w_in_a, b_f_a, w_o_a, w_kv_b, w_in_b, w_o_b, ln_g, ln_b):
    bsz, seq, d_model = x_prompt.shape
    bs, n_tok, _ = x_sample.shape
    n_a, n_b = w_in_a.shape[0], w_in_b.shape[0]
    depth = n_a + n_b
    h_a = b_f_a.shape[1]
    a_width = h_a * HEAD_DIM
    h_b = cache_b0_kv.shape[3]
    b_width = h_b * HEAD_DIM
    qk_width = N_GROUPS * b_width
    alpha = (2.0 * depth) ** 0.25
    scale = HEAD_DIM ** -0.5
    lanes = 128
    b_bufs = (cache_b0_kv, cache_b1_kv, cache_b2_kv)

    w_a = w_in_a[:, :, :4 * a_width].astype(BF16)
    w_f = jnp.pad(w_in_a[:, :, 4 * a_width:], ((0, 0), (0, 0), (0, lanes - h_a))).astype(BF16)
    bf_pad = jnp.pad(b_f_a, ((0, 0), (0, lanes - h_a)))
    w_oa = w_o_a.astype(BF16)
    w_kv = jnp.stack([jnp.concatenate([w_kv_b[:, g * b_width:(g + 1) * b_width],
                                       w_kv_b[:, qk_width + g * b_width:qk_width + (g + 1) * b_width]], axis=1)
                      for g in range(N_GROUPS)]).astype(BF16)
    w_b = w_in_b.astype(BF16)
    w_ob = w_o_b.astype(BF16)

    xp, xp_in = x_prompt.reshape(bsz * seq, d_model), x_prompt.reshape(bsz * seq, d_model)
    xs, xs_in = x_sample.reshape(bs * n_tok, d_model), x_sample.reshape(bs * n_tok, d_model)
    kv_a_p, logf_a_p, kv_a_s, logf_a_s = [], [], [], []

    for layer in range(n_a):
        ln = (ln_g[layer][None], ln_b[layer][None])
        q = _mm(xp_in, w_a, layer, 0, a_width, "scaled_bf16", scale)
        kv, kv_bf = _mm(xp_in, w_a, layer, a_width, 2 * a_width, "f32_and_bf16")
        gate = _mm(xp_in, w_a, layer, 3 * a_width, a_width, "silu")
        fl = _mm(xp_in, w_f, layer, 0, lanes, "f32")
        logf, c, ct = _logf_cumsum(fl, bf_pad[layer][None], bsz, seq, h_a)
        og = _fox_prompt(q, kv_bf, gate, c, ct, bsz, seq, h_a)
        xp, xp_in = _out_ln(og, w_oa, layer, xp, *ln, alpha)
        kv_a_p.append(kv.reshape(bsz, seq, 2, h_a, HEAD_DIM))
        logf_a_p.append(logf)
        q = _mm(xs_in, w_a, layer, 0, a_width, "scaled_bf16", scale)
        kv, _ = _mm(xs_in, w_a, layer, a_width, 2 * a_width, "f32_and_bf16")
        gate = _mm(xs_in, w_a, layer, 3 * a_width, a_width, "silu")
        fl = _mm(xs_in, w_f, layer, 0, lanes, "f32")
        logf = _logf(fl, bf_pad[layer][None], h_a)
        og = _fox_sample(q, kv, logf, gate, cache_a_kv, cache_a_logf, page_table, layer, h_a)
        xs, xs_in = _out_ln(og, w_oa, layer, xs, *ln, alpha)
        kv_a_s.append(kv.reshape(bs, n_tok, 2, h_a, HEAD_DIM))
        logf_a_s.append(logf.reshape(bs, n_tok, h_a))

    kvb_p = [_mm(xp_in, w_kv, g, 0, 2 * b_width, "f32_and_bf16") for g in range(N_GROUPS)]
    kvb_s = [_mm(xs_in, w_kv, g, 0, 2 * b_width, "f32_and_bf16")[0] for g in range(N_GROUPS)]

    for j in range(n_b):
        layer = n_a + j
        ln = (ln_g[layer][None], ln_b[layer][None])
        q = _mm(xp_in, w_b, j, 0, qk_width, "scaled_bf16", scale)
        gate = _mm(xp_in, w_b, j, qk_width, b_width, "silu")
        res = [_dil_prompt(q, kvb_p[g][1], g, bsz, seq, h_b) for g in range(N_GROUPS)]
        a = _merge([r[0] for r in res], [r[1] for r in res], gate, h_b)
        xp, xp_in = _out_ln(a, w_ob, j, xp, *ln, alpha)
        q = _mm(xs_in, w_b, j, 0, qk_width, "scaled_bf16", scale)
        gate = _mm(xs_in, w_b, j, qk_width, b_width, "silu")
        res = [_dil_sample(q, kvb_s[g], b_bufs[g], g, bs, n_tok, h_b) for g in range(N_GROUPS)]
        a = _merge([r[0] for r in res], [r[1] for r in res], gate, h_b)
        xs, xs_in = _out_ln(a, w_ob, j, xs, *ln, alpha)

    new_b_p, new_b_s = [], []
    for g in range(N_GROUPS):
        kv_p = kvb_p[g][0].reshape(bsz, seq, 2, h_b, HEAD_DIM)
        new_b_p.append(kv_p[:, seq - min(B_WINDOWS[g], seq):])
        cat = jnp.concatenate([b_bufs[g], kvb_s[g].reshape(bs, n_tok, 2, h_b, HEAD_DIM)], axis=1)
        new_b_s.append(cat[:, cat.shape[1] - min(B_WINDOWS[g], cat.shape[1]):])

    return (xp.reshape(bsz, seq, d_model), xs.reshape(bs, n_tok, d_model),
            jnp.stack(kv_a_p), jnp.stack(logf_a_p), jnp.stack(kv_a_s), jnp.stack(logf_a_s),
            *new_b_p, *new_b_s)
```

```python
import functools

import jax
import jax.numpy as jnp
from jax import lax
from jax.experimental import pallas as pl
from jax.experimental.pallas import tpu as pltpu

F32 = jnp.float32
BF16 = jnp.bfloat16

HEAD_DIM = 128
PAGE_SIZE = 128
LANES = 128
B_WINDOWS = (128, 512, 2048)
B_DILATIONS = (1, 4, 16)
N_GROUPS = 3
DIL_TILE = 512
LN_EPS = 1e-5
LOG2E = 1.4426950408889634
LN2 = 0.6931471805599453
NEG = -1e30
V7X_VMEM_LIMIT_BYTES = 56 * 1024 * 1024


def _params(*sem):
    return pltpu.CompilerParams(dimension_semantics=sem, vmem_limit_bytes=V7X_VMEM_LIMIT_BYTES)


def _split3(x):
    hi = x.astype(BF16)
    r1 = x - hi.astype(F32)
    mid = r1.astype(BF16)
    lo = (r1 - mid.astype(F32)).astype(BF16)
    return hi, mid, lo


def _dot(a, b):
    return jnp.dot(a, b, preferred_element_type=F32)


def _dot_nt(a, b):
    return lax.dot_general(a, b, (((1,), (1,)), ((), ())), preferred_element_type=F32)


def _dot3(parts, w):
    hi, mid, lo = parts
    return (_dot(hi, w) + _dot(mid, w)) + _dot(lo, w)


def _dot3_left(w, parts):
    hi, mid, lo = parts
    return (_dot(w, hi) + _dot(w, mid)) + _dot(w, lo)


def _div_mod_pow2(x, n):
    assert n & (n - 1) == 0
    return x >> (n.bit_length() - 1), x & (n - 1)


def _iota(shape, dim):
    return lax.broadcasted_iota(jnp.int32, shape, dim)


def _silu(x):
    return x * (1.0 / (1.0 + jnp.exp(-x)))


def _mm_kernel(x_ref, w_ref, *o_refs, mode, scale):
    acc = _dot(x_ref[...].astype(BF16), w_ref[...])
    if mode == "scaled_bf16":
        o_refs[0][...] = (acc * scale).astype(BF16)
    elif mode == "f32_and_bf16":
        o_refs[0][...] = acc
        o_refs[1][...] = acc.astype(BF16)
    elif mode == "silu_bf16":
        o_refs[0][...] = _silu(acc).astype(BF16)
    else:
        o_refs[0][...] = acc


def _mm(x, w, layer, col0, n, mode, scale=1.0):
    m, k = x.shape
    tn = min(n, 1024)
    tm = min(m, 1024)
    assert m % tm == 0 and n % tn == 0 and col0 % tn == 0
    cb = col0 // tn
    out_shape = [jax.ShapeDtypeStruct((m, n), F32 if mode in ("f32", "f32_and_bf16") else BF16)]
    if mode == "f32_and_bf16":
        out_shape.append(jax.ShapeDtypeStruct((m, n), BF16))
    out_specs = [pl.BlockSpec((tm, tn), lambda i, j: (i, j)) for _ in out_shape]
    res = pl.pallas_call(
        functools.partial(_mm_kernel, mode=mode, scale=scale),
        grid=(m // tm, n // tn),
        in_specs=[pl.BlockSpec((tm, k), lambda i, j: (i, 0)),
                  pl.BlockSpec((None, k, tn), lambda i, j: (layer, 0, cb + j))],
        out_specs=out_specs,
        out_shape=out_shape,
        compiler_params=_params("parallel", "parallel"),
        name="proj_" + mode,
    )(x, w)
    return res if mode == "f32_and_bf16" else res[0]


def _out_ln_kernel(a_ref, w_ref, x_ref, g_ref, b_ref, y_ref, ybf_ref, *, alpha):
    mix = _dot(a_ref[...].astype(BF16), w_ref[...])
    z = alpha * x_ref[...] + mix
    mu = jnp.mean(z, axis=-1, keepdims=True)
    zc = z - mu
    var = jnp.mean(zc * zc, axis=-1, keepdims=True)
    y = zc * lax.rsqrt(var + LN_EPS) * g_ref[...] + b_ref[...]
    y_ref[...] = y
    ybf_ref[...] = y.astype(BF16)


def _out_ln(a, w, layer, x, g, b, alpha):
    m, k = a.shape
    d = x.shape[1]
    tm = min(m, 512)
    return pl.pallas_call(
        functools.partial(_out_ln_kernel, alpha=alpha),
        grid=(m // tm,),
        in_specs=[pl.BlockSpec((tm, k), lambda i: (i, 0)),
                  pl.BlockSpec((None, k, d), lambda i: (layer, 0, 0)),
                  pl.BlockSpec((tm, d), lambda i: (i, 0)),
                  pl.BlockSpec((1, d), lambda i: (0, 0)),
                  pl.BlockSpec((1, d), lambda i: (0, 0))],
        out_specs=[pl.BlockSpec((tm, d), lambda i: (i, 0)),
                   pl.BlockSpec((tm, d), lambda i: (i, 0))],
        out_shape=[jax.ShapeDtypeStruct((m, d), F32), jax.ShapeDtypeStruct((m, d), BF16)],
        compiler_params=_params("parallel"),
        name="out_ln",
    )(a, w, x, g, b)


def _log_sigmoid(z):
    return jnp.minimum(z, 0.0) - jnp.log1p(jnp.exp(-jnp.abs(z)))


BIAS_LANES_PER_HEAD = 8


def _logf_cumsum_kernel(fl_ref, bf_ref, logf_ref, eq_ref, ek_ref, *, n_heads, chunk):
    seq = fl_ref.shape[1]
    assert n_heads * BIAS_LANES_PER_HEAD == LANES
    logf = _log_sigmoid(fl_ref[0] + bf_ref[...])
    logf_ref[0] = logf[:, :n_heads]
    tri = jnp.where(_iota((chunk, chunk), 0) >= _iota((chunk, chunk), 1), 1.0, 0.0).astype(BF16)
    src_head = _iota((LANES, LANES), 0)
    dst_head, _ = _div_mod_pow2(_iota((LANES, LANES), 1), BIAS_LANES_PER_HEAD)
    expand = jnp.where(src_head == dst_head, 1.0, 0.0).astype(BF16)
    _, j = _div_mod_pow2(_iota((chunk, LANES), 1), BIAS_LANES_PER_HEAD)
    carry = jnp.zeros((1, LANES), F32)
    for i in range(seq // chunk):
        rows = slice(i * chunk, (i + 1) * chunk)
        cum = _dot3_left(tri, _split3(logf[rows])) + carry
        carry = cum[chunk - 1:chunk]
        c2 = _dot3(_split3(cum), expand) * LOG2E
        hi, mid, lo = (p.astype(F32) for p in _split3(c2))
        first = jnp.where(j == 0, hi, jnp.where(j == 1, mid, lo))
        second = jnp.where(j == 3, hi, jnp.where(j == 4, mid, lo))
        eq_ref[0, rows, :] = jnp.where(j < 3, first, jnp.where(j < 6, 1.0, 0.0)).astype(BF16)
        ek_ref[0, rows, :] = jnp.where(j < 3, 1.0, jnp.where(j < 6, -second, 0.0)).astype(BF16)


def _logf_cumsum(fl, bf_pad, bsz, seq, n_heads):
    lanes = fl.shape[-1]
    return pl.pallas_call(
        functools.partial(_logf_cumsum_kernel, n_heads=n_heads, chunk=256),
        grid=(bsz,),
        in_specs=[pl.BlockSpec((1, seq, lanes), lambda b: (b, 0, 0)),
                  pl.BlockSpec((1, lanes), lambda b: (0, 0))],
        out_specs=[pl.BlockSpec((1, seq, n_heads), lambda b: (b, 0, 0)),
                   pl.BlockSpec((1, seq, lanes), lambda b: (b, 0, 0)),
                   pl.BlockSpec((1, seq, lanes), lambda b: (b, 0, 0))],
        out_shape=[jax.ShapeDtypeStruct((bsz, seq, n_heads), F32),
                   jax.ShapeDtypeStruct((bsz, seq, lanes), BF16),
                   jax.ShapeDtypeStruct((bsz, seq, lanes), BF16)],
        compiler_params=_params("parallel"),
        name="logf_cumsum",
    )(fl.reshape(bsz, seq, lanes), bf_pad)


def _logf_kernel(fl_ref, bf_ref, logf_ref, *, n_heads):
    logf_ref[...] = _log_sigmoid(fl_ref[...] + bf_ref[...])[:, :n_heads]


def _logf(fl, bf_pad, n_heads):
    m, lanes = fl.shape
    return pl.pallas_call(
        functools.partial(_logf_kernel, n_heads=n_heads),
        grid=(1,),
        in_specs=[pl.BlockSpec((m, lanes), lambda i: (0, 0)),
                  pl.BlockSpec((1, lanes), lambda i: (0, 0))],
        out_specs=pl.BlockSpec((m, n_heads), lambda i: (0, 0)),
        out_shape=jax.ShapeDtypeStruct((m, n_heads), F32),
        name="logf",
    )(fl, bf_pad)


def _fox_prompt_kernel(q_ref, eq_ref, k_ref, v_ref, ek_ref, g_ref, o_ref, eqh_sc, m_sc, acc_sc, *, t, n_heads):
    qi = pl.program_id(1)
    kj = pl.program_id(2)

    @pl.when(kj == 0)
    def _():
        m_sc[...] = jnp.full(m_sc.shape, NEG, F32)
        acc_sc[...] = jnp.zeros(acc_sc.shape, F32)
        eq = eq_ref[0].astype(F32)
        lane_head, _ = _div_mod_pow2(_iota(eq.shape, 1), BIAS_LANES_PER_HEAD)
        for h in range(n_heads):
            eqh_sc[h] = jnp.where(lane_head == h, eq, 0.0).astype(BF16)

    @pl.when(kj <= qi)
    def _():
        causal = (_iota((t, t), 0) + qi * t) >= (_iota((t, t), 1) + kj * t)
        ek = ek_ref[0]
        ones_col = jnp.where(_iota((t, LANES), 1) == 0, 1.0, 0.0).astype(BF16)
        for h in range(n_heads):
            sl = slice(h * HEAD_DIM, (h + 1) * HEAD_DIM)
            q_aug = jnp.concatenate([q_ref[0, :, sl], eqh_sc[h]], axis=1)
            k_aug = jnp.concatenate([k_ref[0, :, sl], ek], axis=1)
            v_aug = jnp.concatenate([v_ref[0, :, sl], ones_col], axis=1)
            s = jnp.where(causal, _dot_nt(q_aug, k_aug), NEG)
            m_old = m_sc[h]
            m_new = jnp.maximum(m_old, jnp.max(s, axis=-1, keepdims=True))
            alpha = jnp.exp2(m_old - m_new)
            p = jnp.exp2(s - m_new).astype(BF16)
            acc_sc[h] = alpha * acc_sc[h] + _dot(p, v_aug)
            m_sc[h] = m_new

    @pl.when(kj == qi)
    def _():
        for h in range(n_heads):
            sl = slice(h * HEAD_DIM, (h + 1) * HEAD_DIM)
            acc = acc_sc[h]
            o = acc[:, :HEAD_DIM] * (1.0 / acc[:, HEAD_DIM:HEAD_DIM + 1])
            o_ref[0, :, sl] = (o * g_ref[0, :, sl].astype(F32)).astype(o_ref.dtype)


def _fox_prompt(q, kv_bf, gate, eq, ek, bsz, seq, n_heads):
    a_width = n_heads * HEAD_DIM
    t = 512
    nblk = seq // t
    q_blk = lambda b, i, j: (b, i, 0)
    out = pl.pallas_call(
        functools.partial(_fox_prompt_kernel, t=t, n_heads=n_heads),
        grid=(bsz, nblk, nblk),
        in_specs=[pl.BlockSpec((1, t, a_width), q_blk),
                  pl.BlockSpec((1, t, LANES), q_blk),
                  pl.BlockSpec((1, t, a_width), lambda b, i, j: (b, jnp.minimum(j, i), 0)),
                  pl.BlockSpec((1, t, a_width), lambda b, i, j: (b, jnp.minimum(j, i), 1)),
                  pl.BlockSpec((1, t, LANES), lambda b, i, j: (b, jnp.minimum(j, i), 0)),
                  pl.BlockSpec((1, t, a_width), q_blk)],
        out_specs=pl.BlockSpec((1, t, a_width), q_blk),
        out_shape=jax.ShapeDtypeStruct((bsz, seq, a_width), BF16),
        scratch_shapes=[pltpu.VMEM((n_heads, t, LANES), BF16),
                        pltpu.VMEM((n_heads, t, 1), F32),
                        pltpu.VMEM((n_heads, t, 2 * HEAD_DIM), F32)],
        compiler_params=_params("parallel", "parallel", "arbitrary"),
        name="fox_prompt",
    )(q.reshape(bsz, seq, a_width), eq, kv_bf.reshape(bsz, seq, 2 * a_width),
      kv_bf.reshape(bsz, seq, 2 * a_width), ek, gate.reshape(bsz, seq, a_width))
    return out.reshape(bsz * seq, a_width)


def _fox_sample_kernel(pt_ref, q_ref, lf_ref, kn_ref, vn_ref, g_ref, page_ref, lf16_ref, o_ref,
                       m_sc, l_sc, acc_sc, carry_sc, cq_sc, *, n_heads, n_tok):
    del pt_ref
    s_idx = pl.program_id(1)
    rows = n_heads * n_tok
    assert rows == LANES and PAGE_SIZE * n_heads == 16 * LANES
    row = _iota((rows, LANES), 0)
    lane = _iota((rows, LANES), 1)
    h_r, t_r = _div_mod_pow2(row, n_tok)
    g_l, h_l = _div_mod_pow2(lane, n_heads)
    same_head = h_l == h_r
    q = q_ref[0]

    @pl.when(s_idx == 0)
    def _():
        lf = lf_ref[0]
        cnew = jnp.sum(jnp.where(same_head & (g_l <= t_r), lf, 0.0), axis=-1, keepdims=True)
        bias = jnp.zeros((rows, LANES), F32)
        for s in range(n_tok):
            c_s = jnp.sum(jnp.where(same_head & (g_l <= s), lf, 0.0), axis=-1, keepdims=True)
            bias = jnp.where(g_l == s, cnew - c_s, bias)
        valid = same_head & (g_l <= t_r)
        sc = jnp.where(valid, _dot_nt(q, kn_ref[0].astype(BF16)) + bias * LOG2E, NEG)
        m = jnp.max(sc, axis=-1, keepdims=True)
        p = jnp.exp2(sc - m)
        m_sc[...] = m
        l_sc[...] = jnp.sum(p, axis=-1, keepdims=True)
        acc_sc[...] = _dot(p.astype(BF16), vn_ref[0].astype(BF16))
        carry_sc[...] = jnp.zeros_like(carry_sc)
        cq_sc[...] = cnew

    @pl.when(s_idx > 0)
    def _():
        src = _iota((LANES, LANES), 0)
        dst = _iota((LANES, LANES), 1)
        s_g, s_h = _div_mod_pow2(src, n_heads)
        d_g, d_h = _div_mod_pow2(dst, n_heads)
        later_in_row = jnp.where((s_h == d_h) & (s_g > d_g), 1.0, 0.0).astype(BF16)
        whole_row = jnp.where(s_h == d_h, 1.0, 0.0).astype(BF16)
        parts = _split3(lf16_ref[...])
        in_row = _dot3(parts, later_in_row)
        row_tot = _dot3(parts, whole_row)
        r_idx = _iota(row_tot.shape, 0)
        after = jnp.zeros_like(row_tot)
        for r2 in range(1, row_tot.shape[0]):
            after = after + jnp.where(r_idx < r2, row_tot[r2:r2 + 1, :], 0.0)
        bias16 = (in_row + after + carry_sc[...]) * LOG2E
        carry_sc[...] = carry_sc[...] + jnp.sum(row_tot, axis=0, keepdims=True)
        cq = cq_sc[...] * LOG2E

        n_kv_rows = PAGE_SIZE * n_heads
        k2 = page_ref[:, 0].reshape(n_kv_rows, HEAD_DIM).astype(BF16)
        v2 = page_ref[:, 1].reshape(n_kv_rows, HEAD_DIM).astype(BF16)
        s = _dot_nt(q, k2)
        tiles = [jnp.where(same_head, s[:, j * LANES:(j + 1) * LANES] + bias16[j:j + 1, :] + cq, NEG)
                 for j in range(n_kv_rows // LANES)]
        tile_max = functools.reduce(jnp.maximum, tiles)
        m_old = m_sc[...]
        m_new = jnp.maximum(m_old, jnp.max(tile_max, axis=-1, keepdims=True))
        alpha = jnp.exp2(m_old - m_new)
        ps = [jnp.exp2(tile - m_new) for tile in tiles]
        p_sum = functools.reduce(lambda a, b: a + b, ps)
        m_sc[...] = m_new
        l_sc[...] = alpha * l_sc[...] + jnp.sum(p_sum, axis=-1, keepdims=True)
        p = jnp.concatenate([x.astype(BF16) for x in ps], axis=1)
        acc_sc[...] = alpha * acc_sc[...] + _dot(p, v2)

    @pl.when(s_idx == pl.num_programs(1) - 1)
    def _():
        o = acc_sc[...] * (1.0 / l_sc[...])
        for h in range(n_heads):
            sl = slice(h * HEAD_DIM, (h + 1) * HEAD_DIM)
            o_ref[0, :, sl] = o[h * n_tok:(h + 1) * n_tok, :] * g_ref[0, :, sl].astype(F32)


def _fox_sample(q_bf, kv_new, logf_new, gate, cache_kv, cache_lf16, page_table, layer, n_heads):
    bs, n_pages = page_table.shape
    n_tok = q_bf.shape[0] // bs
    a_width = n_heads * HEAD_DIM
    rows = n_heads * n_tok
    qr = q_bf.reshape(bs, n_tok, n_heads, HEAD_DIM).transpose(0, 2, 1, 3).reshape(bs, rows, HEAD_DIM)
    lf = logf_new.reshape(bs, 1, n_tok * n_heads)
    kv5 = kv_new.reshape(bs, n_tok, 2, n_heads, HEAD_DIM)
    k_new = kv5[:, :, 0].reshape(bs, rows, HEAD_DIM)
    v_new = kv5[:, :, 1].reshape(bs, rows, HEAD_DIM)

    def page_idx(b, s, pt):
        return (layer, pt[b, n_pages - jnp.maximum(s, 1)], 0, 0, 0, 0)

    def lf16_idx(b, s, pt):
        return (layer, pt[b, n_pages - jnp.maximum(s, 1)], 0, 0)

    per_seq = lambda b, s, pt: (b, 0, 0)
    out = pl.pallas_call(
        functools.partial(_fox_sample_kernel, n_heads=n_heads, n_tok=n_tok),
        grid_spec=pltpu.PrefetchScalarGridSpec(
            num_scalar_prefetch=1,
            grid=(bs, n_pages + 1),
            in_specs=[pl.BlockSpec((1, rows, HEAD_DIM), per_seq),
                      pl.BlockSpec((1, 1, n_tok * n_heads), per_seq),
                      pl.BlockSpec((1, rows, HEAD_DIM), per_seq),
                      pl.BlockSpec((1, rows, HEAD_DIM), per_seq),
                      pl.BlockSpec((1, n_tok, a_width), per_seq),
                      pl.BlockSpec((None, None, PAGE_SIZE, 2, n_heads, HEAD_DIM), page_idx),
                      pl.BlockSpec((None, None, 16, LANES), lf16_idx)],
            out_specs=pl.BlockSpec((1, n_tok, a_width), per_seq),
            scratch_shapes=[pltpu.VMEM((rows, 1), F32),
                            pltpu.VMEM((rows, 1), F32),
                            pltpu.VMEM((rows, HEAD_DIM), F32),
                            pltpu.VMEM((1, LANES), F32),
                            pltpu.VMEM((rows, 1), F32)]),
        out_shape=jax.ShapeDtypeStruct((bs, n_tok, a_width), F32),
        compiler_params=_params("parallel", "arbitrary"),
        name="fox_sample",
    )(page_table, qr, lf, k_new, v_new, gate.reshape(bs, n_tok, a_width), cache_kv, cache_lf16)
    return out.reshape(bs * n_tok, a_width)


def _alibi_slopes(group, n_heads):
    n = N_GROUPS * n_heads
    return tuple(2.0 ** (-8.0 * (group * n_heads + h + 1) / n) for h in range(n_heads))


def _dil_steps(group, n_q_tiles):
    counts = [min(-(-B_WINDOWS[g] // DIL_TILE) + 1, n_q_tiles) for g in range(N_GROUPS)]
    return sum(counts[:group]), counts[group]


def _dil_key_tile(group, n_q_tiles, qi, step):
    first, count = _dil_steps(group, n_q_tiles)
    return qi - jnp.clip(step - first, 0, count - 1)


def _dil_prompt_kernel(q0_ref, q1_ref, q2_ref, k0_ref, v0_ref, k1_ref, v1_ref, k2_ref, v2_ref, g_ref, o_ref,
                       m_sc, acc_sc, *, t, n_heads, n_q_tiles):
    qi = pl.program_id(1)
    step = pl.program_id(2)
    q_refs = (q0_ref, q1_ref, q2_ref)
    k_refs = (k0_ref, k1_ref, k2_ref)
    v_refs = (v0_ref, v1_ref, v2_ref)

    @pl.when(step == 0)
    def _():
        m_sc[...] = jnp.full(m_sc.shape, NEG, F32)
        acc_sc[...] = jnp.zeros(acc_sc.shape, F32)

    for group in range(N_GROUPS):
        first, count = _dil_steps(group, n_q_tiles)
        kt = qi - (step - first)
        window, dil = B_WINDOWS[group], B_DILATIONS[group]

        @pl.when((step >= first) & (step < first + count) & (kt >= 0))
        def _(group=group, kt=kt, window=window, dil=dil):
            delta = (qi - kt) * t + _iota((t, t), 0) - _iota((t, t), 1)
            valid = (delta >= 0) & (delta <= window) & ((delta & (dil - 1)) == 0)
            delta_f = delta.astype(F32)
            ones_col = jnp.where(_iota((t, LANES), 1) == 0, 1.0, 0.0).astype(BF16)
            for h, slope in enumerate(_alibi_slopes(group, n_heads)):
                sl = slice(h * HEAD_DIM, (h + 1) * HEAD_DIM)
                bias = jnp.where(valid, (-slope * LOG2E) * delta_f, NEG)
                s = _dot_nt(q_refs[group][0, :, sl], k_refs[group][0, :, sl]) + bias
                v_aug = jnp.concatenate([v_refs[group][0, :, sl], ones_col], axis=1)
                m_old = m_sc[h]
                m_new = jnp.maximum(m_old, jnp.max(s, axis=-1, keepdims=True))
                alpha = jnp.exp2(m_old - m_new)
                p = jnp.exp2(s - m_new).astype(BF16)
                acc_sc[h] = alpha * acc_sc[h] + _dot(p, v_aug)
                m_sc[h] = m_new

    @pl.when(step == pl.num_programs(2) - 1)
    def _():
        for h in range(n_heads):
            sl = slice(h * HEAD_DIM, (h + 1) * HEAD_DIM)
            acc = acc_sc[h]
            o = acc[:, :HEAD_DIM] * (1.0 / acc[:, HEAD_DIM:HEAD_DIM + 1])
            o_ref[0, :, sl] = (o * g_ref[0, :, sl].astype(F32)).astype(o_ref.dtype)


def _dil_prompt(q_bf, kv_bfs, gate, bsz, seq, n_heads):
    width = n_heads * HEAD_DIM
    t = DIL_TILE
    assert seq % t == 0 and all(d & (d - 1) == 0 for d in B_DILATIONS)
    n_q_tiles = seq // t
    n_steps = sum(_dil_steps(g, n_q_tiles)[1] for g in range(N_GROUPS))
    blk = (1, t, width)
    q_tile = lambda b, i, s: (b, i, 0)

    def q_spec(group):
        return pl.BlockSpec(blk, lambda b, i, s: (b, i, group))

    def kv_spec(group, col):
        def idx(b, i, s):
            return (b, jnp.maximum(_dil_key_tile(group, n_q_tiles, i, s), 0), col)
        return pl.BlockSpec(blk, idx)

    q3 = q_bf.reshape(bsz, seq, N_GROUPS * width)
    kvs = [kv.reshape(bsz, seq, 2 * width) for kv in kv_bfs]
    out = pl.pallas_call(
        functools.partial(_dil_prompt_kernel, t=t, n_heads=n_heads, n_q_tiles=n_q_tiles),
        grid=(bsz, n_q_tiles, n_steps),
        in_specs=[q_spec(0), q_spec(1), q_spec(2),
                  kv_spec(0, 0), kv_spec(0, 1), kv_spec(1, 0), kv_spec(1, 1), kv_spec(2, 0), kv_spec(2, 1),
                  pl.BlockSpec(blk, q_tile)],
        out_specs=pl.BlockSpec(blk, q_tile),
        out_shape=jax.ShapeDtypeStruct((bsz, seq, width), BF16),
        scratch_shapes=[pltpu.VMEM((n_heads, t, 1), F32),
                        pltpu.VMEM((n_heads, t, 2 * HEAD_DIM), F32)],
        compiler_params=_params("parallel", "parallel", "arbitrary"),
        name="dil_prompt",
    )(q3, q3, q3, kvs[0], kvs[0], kvs[1], kvs[1], kvs[2], kvs[2], gate.reshape(bsz, seq, width))
    return out.reshape(bsz * seq, width)


def _block_diag_queries(qr, qbd_sc, n_heads, n_tok):
    rows = qr.shape[0]
    head_of_row, _ = _div_mod_pow2(_iota((rows, HEAD_DIM), 0), n_tok)
    for h in range(n_heads):
        qbd_sc[:, h * HEAD_DIM:(h + 1) * HEAD_DIM] = jnp.where(head_of_row == h, qr, jnp.zeros_like(qr))


def _softmax_first(s, v, m_sc, l_sc, acc_sc):
    m = jnp.max(s, axis=-1, keepdims=True)
    p = jnp.exp2(s - m)
    m_sc[...] = m
    l_sc[...] = jnp.sum(p, axis=-1, keepdims=True)
    acc_sc[...] = _dot(p.astype(BF16), v)


def _softmax_next(s, v, m_sc, l_sc, acc_sc):
    m_old = m_sc[...]
    m_new = jnp.maximum(m_old, jnp.max(s, axis=-1, keepdims=True))
    alpha = jnp.exp2(m_old - m_new)
    p = jnp.exp2(s - m_new)
    m_sc[...] = m_new
    l_sc[...] = alpha * l_sc[...] + jnp.sum(p, axis=-1, keepdims=True)
    acc_sc[...] = alpha * acc_sc[...] + _dot(p.astype(BF16), v)


def _dil_sample_kernel(qr_ref, new_ref, buf_ref, o_ref, lse_ref, qbd_sc, m_sc, l_sc, acc_sc,
                       *, window, dil, chunk, slopes, n_tok):
    s_idx = pl.program_id(1)
    n_heads = len(slopes)
    width = n_heads * HEAD_DIM
    rows = n_heads * n_tok

    def geometry(lanes):
        h_r, t_r = _div_mod_pow2(_iota((rows, lanes), 0), n_tok)
        slope = jnp.zeros((rows, lanes), F32)
        for h, sv in enumerate(slopes):
            slope = jnp.where(h_r == h, sv * LOG2E, slope)
        return t_r, _iota((rows, lanes), 1), slope

    def scores(kv, delta, valid, slope):
        k = kv[:, :width].astype(BF16)
        v = kv[:, width:].astype(BF16)
        s = _dot_nt(qbd_sc[...], k) - slope * delta.astype(F32)
        return jnp.where(valid, s, NEG), v

    @pl.when(s_idx == 0)
    def _():
        _block_diag_queries(qr_ref[0], qbd_sc, n_heads, n_tok)
        t_r, lane, slope = geometry(new_ref.shape[1])
        delta = t_r - lane
        valid = (delta >= 0) & ((delta & (dil - 1)) == 0)
        s, v = scores(new_ref[0], delta, valid, slope)
        _softmax_first(s, v, m_sc, l_sc, acc_sc)

    @pl.when(s_idx > 0)
    def _():
        t_r, lane, slope = geometry(chunk)
        delta = window + t_r - ((s_idx - 1) * chunk + lane)
        valid = ((delta & (dil - 1)) == 0) & (delta <= window)
        s, v = scores(buf_ref[0], delta, valid, slope)
        _softmax_next(s, v, m_sc, l_sc, acc_sc)

    @pl.when(s_idx == pl.num_programs(1) - 1)
    def _():
        l = l_sc[...]
        inv_l = 1.0 / l
        for h in range(n_heads):
            sl = slice(h * HEAD_DIM, (h + 1) * HEAD_DIM)
            rs = slice(h * n_tok, (h + 1) * n_tok)
            o_ref[0, :, sl] = acc_sc[rs, sl] * inv_l[rs]
        lse_ref[0] = m_sc[...] * LN2 + jnp.log(l)


def _dil_sample(q_bf, kv_new, buf, group, bs, n_tok, n_heads):
    window, dil = B_WINDOWS[group], B_DILATIONS[group]
    assert dil & (dil - 1) == 0
    width = n_heads * HEAD_DIM
    rows = n_heads * n_tok
    buf_len = buf.shape[1]
    assert buf_len == window, "the sample path assumes a full window buffer"
    chunk = min(buf_len, 512)
    new_rows = 128
    qg = q_bf[:, group * width:(group + 1) * width]
    qr = qg.reshape(bs, n_tok, n_heads, HEAD_DIM).transpose(0, 2, 1, 3).reshape(bs, rows, HEAD_DIM)
    new_pad = jnp.pad(kv_new.reshape(bs, n_tok, 2 * width), ((0, 0), (0, new_rows - n_tok), (0, 0)))
    per_seq = lambda b, s: (b, 0, 0)
    o, lse = pl.pallas_call(
        functools.partial(_dil_sample_kernel, window=window, dil=dil, chunk=chunk,
                          slopes=_alibi_slopes(group, n_heads), n_tok=n_tok),
        grid=(bs, buf_len // chunk + 1),
        in_specs=[pl.BlockSpec((1, rows, HEAD_DIM), per_seq),
                  pl.BlockSpec((1, new_rows, 2 * width), per_seq),
                  pl.BlockSpec((1, chunk, 2 * width), lambda b, s: (b, jnp.maximum(s - 1, 0), 0))],
        out_specs=[pl.BlockSpec((1, n_tok, width), per_seq),
                   pl.BlockSpec((1, rows, 1), per_seq)],
        out_shape=[jax.ShapeDtypeStruct((bs, n_tok, width), F32),
                   jax.ShapeDtypeStruct((bs, rows, 1), F32)],
        scratch_shapes=[pltpu.VMEM((rows, width), BF16),
                        pltpu.VMEM((rows, 1), F32),
                        pltpu.VMEM((rows, 1), F32),
                        pltpu.VMEM((rows, width), F32)],
        compiler_params=_params("parallel", "arbitrary"),
        name=f"dil_sample_g{group}",
    )(qr, new_pad, buf.reshape(bs, buf_len, 2 * width))
    lse = lse.reshape(bs, n_heads, n_tok).transpose(0, 2, 1).reshape(bs * n_tok, n_heads)
    return o.reshape(bs * n_tok, width), lse


def _merge_kernel(o0_ref, o1_ref, o2_ref, l0_ref, l1_ref, l2_ref, g_ref, a_ref, *, n_heads):
    l0, l1, l2 = l0_ref[...], l1_ref[...], l2_ref[...]
    mx = jnp.maximum(jnp.maximum(l0, l1), l2)
    e0, e1, e2 = jnp.exp(l0 - mx), jnp.exp(l1 - mx), jnp.exp(l2 - mx)
    inv = 1.0 / (e0 + e1 + e2)
    w0, w1, w2 = e0 * inv, e1 * inv, e2 * inv
    for h in range(n_heads):
        sl = slice(h * HEAD_DIM, (h + 1) * HEAD_DIM)
        hs = slice(h, h + 1)
        y = w0[:, hs] * o0_ref[:, sl] + w1[:, hs] * o1_ref[:, sl] + w2[:, hs] * o2_ref[:, sl]
        a_ref[:, sl] = (y * g_ref[:, sl].astype(F32)).astype(a_ref.dtype)


def _merge(outs, lses, gate, n_heads):
    m, width = gate.shape
    tm = min(m, 512)
    big = pl.BlockSpec((tm, width), lambda i: (i, 0))
    small = pl.BlockSpec((tm, n_heads), lambda i: (i, 0))
    return pl.pallas_call(
        functools.partial(_merge_kernel, n_heads=n_heads),
        grid=(m // tm,),
        in_specs=[big, big, big, small, small, small, big],
        out_specs=big,
        out_shape=jax.ShapeDtypeStruct((m, width), BF16),
        compiler_params=_params("parallel"),
        name="merge_groups",
    )(*outs, *lses, gate)


def kernel(x_prompt, x_sample, cache_a_kv, cache_a_logf, cache_b0_kv, cache_b1_kv, cache_b2_kv, page_table,
           w_in_a, b_f_a, w_o_a, w_kv_b, w_in_b, w_o_b, ln_g, ln_b):
    bsz, seq, d_model = x_prompt.shape
    bs, n_tok, _ = x_sample.shape
    n_a, n_b = w_in_a.shape[0], w_in_b.shape[0]
    depth = n_a + n_b
    h_a = b_f_a.shape[1]
    a_width = h_a * HEAD_DIM
    h_b = cache_b0_kv.shape[3]
    b_width = h_b * HEAD_DIM
    qk_width = N_GROUPS * b_width
    alpha = (2.0 * depth) ** 0.25
    scale = HEAD_DIM ** -0.5
    b_bufs = (cache_b0_kv, cache_b1_kv, cache_b2_kv)

    w_a = w_in_a[:, :, :4 * a_width].astype(BF16)
    w_f = jnp.pad(w_in_a[:, :, 4 * a_width:], ((0, 0), (0, 0), (0, LANES - h_a))).astype(BF16)
    bf_pad = jnp.pad(b_f_a, ((0, 0), (0, LANES - h_a)))
    w_oa = w_o_a.astype(BF16)
    w_kv = jnp.stack([jnp.concatenate([w_kv_b[:, g * b_width:(g + 1) * b_width],
                                       w_kv_b[:, qk_width + g * b_width:qk_width + (g + 1) * b_width]], axis=1)
                      for g in range(N_GROUPS)]).astype(BF16)
    w_b = w_in_b.astype(BF16)
    w_ob = w_o_b.astype(BF16)
    cache_lf16 = cache_a_logf.reshape(n_a, cache_a_logf.shape[1], 16, LANES)

    xp = x_prompt.reshape(bsz * seq, d_model)
    xs = x_sample.reshape(bs * n_tok, d_model)
    xp_in, xs_in = xp.astype(BF16), xs.astype(BF16)
    kv_a_p, logf_a_p, kv_a_s, logf_a_s = [], [], [], []

    for layer in range(n_a):
        ln = (ln_g[layer][None], ln_b[layer][None])
        q = _mm(xp_in, w_a, layer, 0, a_width, "scaled_bf16", scale * LOG2E)
        kv, kv_bf = _mm(xp_in, w_a, layer, a_width, 2 * a_width, "f32_and_bf16")
        gate = _mm(xp_in, w_a, layer, 3 * a_width, a_width, "silu_bf16")
        fl = _mm(xp_in, w_f, layer, 0, LANES, "f32")
        logf, eq, ek = _logf_cumsum(fl, bf_pad[layer][None], bsz, seq, h_a)
        og = _fox_prompt(q, kv_bf, gate, eq, ek, bsz, seq, h_a)
        xp, xp_in = _out_ln(og, w_oa, layer, xp, *ln, alpha)
        kv_a_p.append(kv.reshape(bsz, seq, 2, h_a, HEAD_DIM))
        logf_a_p.append(logf)
        q = _mm(xs_in, w_a, layer, 0, a_width, "scaled_bf16", scale * LOG2E)
        kv, _ = _mm(xs_in, w_a, layer, a_width, 2 * a_width, "f32_and_bf16")
        gate = _mm(xs_in, w_a, layer, 3 * a_width, a_width, "silu_bf16")
        fl = _mm(xs_in, w_f, layer, 0, LANES, "f32")
        logf = _logf(fl, bf_pad[layer][None], h_a)
        og = _fox_sample(q, kv, logf, gate, cache_a_kv, cache_lf16, page_table, layer, h_a)
        xs, xs_in = _out_ln(og, w_oa, layer, xs, *ln, alpha)
        kv_a_s.append(kv.reshape(bs, n_tok, 2, h_a, HEAD_DIM))
        logf_a_s.append(logf.reshape(bs, n_tok, h_a))

    kvb_p = [_mm(xp_in, w_kv, g, 0, 2 * b_width, "f32_and_bf16") for g in range(N_GROUPS)]
    kvb_s = [_mm(xs_in, w_kv, g, 0, 2 * b_width, "f32_and_bf16")[0] for g in range(N_GROUPS)]

    for j in range(n_b):
        layer = n_a + j
        ln = (ln_g[layer][None], ln_b[layer][None])
        q = _mm(xp_in, w_b, j, 0, qk_width, "scaled_bf16", scale * LOG2E)
        gate = _mm(xp_in, w_b, j, qk_width, b_width, "silu_bf16")
        a = _dil_prompt(q, [kvb_p[g][1] for g in range(N_GROUPS)], gate, bsz, seq, h_b)
        xp, xp_in = _out_ln(a, w_ob, j, xp, *ln, alpha)
        q = _mm(xs_in, w_b, j, 0, qk_width, "scaled_bf16", scale * LOG2E)
        gate = _mm(xs_in, w_b, j, qk_width, b_width, "silu_bf16")
        res = [_dil_sample(q, kvb_s[g], b_bufs[g], g, bs, n_tok, h_b) for g in range(N_GROUPS)]
        a = _merge([r[0] for r in res], [r[1] for r in res], gate, h_b)
        xs, xs_in = _out_ln(a, w_ob, j, xs, *ln, alpha)

    new_b_p, new_b_s = [], []
    for g in range(N_GROUPS):
        kv_p = kvb_p[g][0].reshape(bsz, seq, 2, h_b, HEAD_DIM)
        new_b_p.append(kv_p[:, seq - min(B_WINDOWS[g], seq):])
        cat = jnp.concatenate([b_bufs[g], kvb_s[g].reshape(bs, n_tok, 2, h_b, HEAD_DIM)], axis=1)
        new_b_s.append(cat[:, cat.shape[1] - min(B_WINDOWS[g], cat.shape[1]):])

    return (xp.reshape(bsz, seq, d_model), xs.reshape(bs, n_tok, d_model),
            jnp.stack(kv_a_p), jnp.stack(logf_a_p), jnp.stack(kv_a_s), jnp.stack(logf_a_s),
            *new_b_p, *new_b_s)
```

```python
import functools

import jax
import jax.numpy as jnp
from jax import lax
from jax.experimental import pallas as pl
from jax.experimental.pallas import tpu as pltpu

F32 = jnp.float32
BF16 = jnp.bfloat16

HEAD_DIM = 128
PAGE_SIZE = 128
LANES = 128
B_WINDOWS = (128, 512, 2048)
B_DILATIONS = (1, 4, 16)
N_GROUPS = 3
DIL_TILE = 512
MM_TILE = 1024
LN_EPS = 1e-5
LOG2E = 1.4426950408889634
LN2 = 0.6931471805599453
NEG = -1e30
V7X_VMEM_LIMIT_BYTES = 56 * 1024 * 1024


def _params(*sem):
    return pltpu.CompilerParams(dimension_semantics=sem, vmem_limit_bytes=V7X_VMEM_LIMIT_BYTES)


def _split3(x):
    hi = x.astype(BF16)
    r1 = x - hi.astype(F32)
    mid = r1.astype(BF16)
    lo = (r1 - mid.astype(F32)).astype(BF16)
    return hi, mid, lo


def _dot(a, b):
    return jnp.dot(a, b, preferred_element_type=F32)


def _dot_nt(a, b):
    return lax.dot_general(a, b, (((1,), (1,)), ((), ())), preferred_element_type=F32)


def _dot3(parts, w):
    hi, mid, lo = parts
    return (_dot(hi, w) + _dot(mid, w)) + _dot(lo, w)


def _dot3_left(w, parts):
    hi, mid, lo = parts
    return (_dot(w, hi) + _dot(w, mid)) + _dot(w, lo)


def _div_mod_pow2(x, n):
    assert n & (n - 1) == 0
    return x >> (n.bit_length() - 1), x & (n - 1)


def _iota(shape, dim):
    return lax.broadcasted_iota(jnp.int32, shape, dim)


def _silu(x):
    return x * (1.0 / (1.0 + jnp.exp(-x)))


def _mm_kernel(x_ref, w_ref, *refs, mode, scale):
    *o_refs, w_sc = refs

    @pl.when(pl.program_id(1) == 0)
    def _():
        w_sc[...] = w_ref[...].astype(BF16)

    acc = _dot(x_ref[...], w_sc[...])
    if mode == "scaled_bf16":
        o_refs[0][...] = (acc * scale).astype(BF16)
    elif mode == "f32_and_bf16":
        o_refs[0][...] = acc
        o_refs[1][...] = acc.astype(BF16)
    elif mode == "silu_bf16":
        o_refs[0][...] = _silu(acc).astype(BF16)
    else:
        o_refs[0][...] = acc


def _mm(x, w, layer, col0, n, mode, scale=1.0, col_stride=1):
    m, k = x.shape
    tn = min(n, MM_TILE)
    tm = min(m, MM_TILE)
    assert m % tm == 0 and n % tn == 0 and col0 % tn == 0
    cb = col0 // tn
    out_shape = [jax.ShapeDtypeStruct((m, n), F32 if mode in ("f32", "f32_and_bf16") else BF16)]
    if mode == "f32_and_bf16":
        out_shape.append(jax.ShapeDtypeStruct((m, n), BF16))
    out_specs = [pl.BlockSpec((tm, tn), lambda j, i: (i, j)) for _ in out_shape]
    res = pl.pallas_call(
        functools.partial(_mm_kernel, mode=mode, scale=scale),
        grid=(n // tn, m // tm),
        in_specs=[pl.BlockSpec((tm, k), lambda j, i: (i, 0)),
                  pl.BlockSpec((None, k, tn), lambda j, i: (layer, 0, cb + j * col_stride))],
        out_specs=out_specs,
        out_shape=out_shape,
        scratch_shapes=[pltpu.VMEM((k, tn), BF16)],
        compiler_params=_params("parallel", "arbitrary"),
        name="proj_" + mode,
    )(x, w)
    return res if mode == "f32_and_bf16" else res[0]


def _out_ln_kernel(a_ref, w_ref, x_ref, g_ref, b_ref, y_ref, ybf_ref, *, alpha):
    mix = _dot(a_ref[...].astype(BF16), w_ref[...])
    z = alpha * x_ref[...] + mix
    mu = jnp.mean(z, axis=-1, keepdims=True)
    zc = z - mu
    var = jnp.mean(zc * zc, axis=-1, keepdims=True)
    y = zc * lax.rsqrt(var + LN_EPS) * g_ref[...] + b_ref[...]
    y_ref[...] = y
    ybf_ref[...] = y.astype(BF16)


def _out_ln(a, w, layer, x, g, b, alpha):
    m, k = a.shape
    d = x.shape[1]
    tm = min(m, 512)
    return pl.pallas_call(
        functools.partial(_out_ln_kernel, alpha=alpha),
        grid=(m // tm,),
        in_specs=[pl.BlockSpec((tm, k), lambda i: (i, 0)),
                  pl.BlockSpec((None, k, d), lambda i: (layer, 0, 0)),
                  pl.BlockSpec((tm, d), lambda i: (i, 0)),
                  pl.BlockSpec((1, d), lambda i: (0, 0)),
                  pl.BlockSpec((1, d), lambda i: (0, 0))],
        out_specs=[pl.BlockSpec((tm, d), lambda i: (i, 0)),
                   pl.BlockSpec((tm, d), lambda i: (i, 0))],
        out_shape=[jax.ShapeDtypeStruct((m, d), F32), jax.ShapeDtypeStruct((m, d), BF16)],
        compiler_params=_params("parallel"),
        name="out_ln",
    )(a, w, x, g, b)


def _log_sigmoid(z):
    return jnp.minimum(z, 0.0) - jnp.log1p(jnp.exp(-jnp.abs(z)))


BIAS_LANES_PER_HEAD = 8


def _logf_cumsum_kernel(fl_ref, bf_ref, logf_ref, eq_ref, ek_ref, *, n_heads, chunk):
    seq = fl_ref.shape[1]
    assert n_heads * BIAS_LANES_PER_HEAD == LANES
    logf = _log_sigmoid(fl_ref[0] + bf_ref[...])
    logf_ref[0] = logf[:, :n_heads]
    tri = jnp.where(_iota((chunk, chunk), 0) >= _iota((chunk, chunk), 1), 1.0, 0.0).astype(BF16)
    src_head = _iota((LANES, LANES), 0)
    dst_head, _ = _div_mod_pow2(_iota((LANES, LANES), 1), BIAS_LANES_PER_HEAD)
    expand = jnp.where(src_head == dst_head, 1.0, 0.0).astype(BF16)
    _, j = _div_mod_pow2(_iota((chunk, LANES), 1), BIAS_LANES_PER_HEAD)
    carry = jnp.zeros((1, LANES), F32)
    for i in range(seq // chunk):
        rows = slice(i * chunk, (i + 1) * chunk)
        cum = _dot3_left(tri, _split3(logf[rows])) + carry
        carry = cum[chunk - 1:chunk]
        c2 = _dot3(_split3(cum), expand) * LOG2E
        hi, mid, lo = (p.astype(F32) for p in _split3(c2))
        first = jnp.where(j == 0, hi, jnp.where(j == 1, mid, lo))
        second = jnp.where(j == 3, hi, jnp.where(j == 4, mid, lo))
        eq_ref[0, rows, :] = jnp.where(j < 3, first, jnp.where(j < 6, 1.0, 0.0)).astype(BF16)
        ek_ref[0, rows, :] = jnp.where(j < 3, 1.0, jnp.where(j < 6, -second, 0.0)).astype(BF16)


def _logf_cumsum(fl, bf_pad, bsz, seq, n_heads):
    lanes = fl.shape[-1]
    return pl.pallas_call(
        functools.partial(_logf_cumsum_kernel, n_heads=n_heads, chunk=256),
        grid=(bsz,),
        in_specs=[pl.BlockSpec((1, seq, lanes), lambda b: (b, 0, 0)),
                  pl.BlockSpec((1, lanes), lambda b: (0, 0))],
        out_specs=[pl.BlockSpec((1, seq, n_heads), lambda b: (b, 0, 0)),
                   pl.BlockSpec((1, seq, lanes), lambda b: (b, 0, 0)),
                   pl.BlockSpec((1, seq, lanes), lambda b: (b, 0, 0))],
        out_shape=[jax.ShapeDtypeStruct((bsz, seq, n_heads), F32),
                   jax.ShapeDtypeStruct((bsz, seq, lanes), BF16),
                   jax.ShapeDtypeStruct((bsz, seq, lanes), BF16)],
        compiler_params=_params("parallel"),
        name="logf_cumsum",
    )(fl.reshape(bsz, seq, lanes), bf_pad)


def _logf_kernel(fl_ref, bf_ref, logf_ref, *, n_heads):
    logf_ref[...] = _log_sigmoid(fl_ref[...] + bf_ref[...])[:, :n_heads]


def _logf(fl, bf_pad, n_heads):
    m, lanes = fl.shape
    return pl.pallas_call(
        functools.partial(_logf_kernel, n_heads=n_heads),
        grid=(1,),
        in_specs=[pl.BlockSpec((m, lanes), lambda i: (0, 0)),
                  pl.BlockSpec((1, lanes), lambda i: (0, 0))],
        out_specs=pl.BlockSpec((m, n_heads), lambda i: (0, 0)),
        out_shape=jax.ShapeDtypeStruct((m, n_heads), F32),
        name="logf",
    )(fl, bf_pad)


def _flash_update(s, v_aug, m_ref, acc_ref):
    m_old = m_ref[...]
    m_new = jnp.maximum(m_old, jnp.max(s, axis=-1, keepdims=True))
    alpha = jnp.exp2(m_old - m_new)
    p = jnp.exp2(s - jnp.concatenate([m_new] * (s.shape[1] // LANES), axis=1)).astype(BF16)
    alpha_wide = jnp.concatenate([alpha] * (acc_ref.shape[1] // LANES), axis=1)
    acc_ref[...] = alpha_wide * acc_ref[...] + _dot(p, v_aug)
    m_ref[...] = m_new


def _fox_prompt_kernel(q_ref, eq_ref, k_ref, v_ref, ek_ref, g_ref, o_ref, eqh_sc, m_sc, acc_sc, *, t, n_heads):
    qi = pl.program_id(1)
    kj = pl.program_id(2)

    @pl.when(kj == 0)
    def _():
        m_sc[...] = jnp.full(m_sc.shape, NEG, F32)
        acc_sc[...] = jnp.zeros(acc_sc.shape, F32)
        eq = eq_ref[0].astype(F32)
        lane_head, _ = _div_mod_pow2(_iota(eq.shape, 1), BIAS_LANES_PER_HEAD)
        for h in range(n_heads):
            eqh_sc[h] = jnp.where(lane_head == h, eq, 0.0).astype(BF16)

    @pl.when(kj <= qi)
    def _():
        causal = (_iota((t, t), 0) + qi * t) >= (_iota((t, t), 1) + kj * t)
        ek = ek_ref[0]
        ones_col = jnp.where(_iota((t, LANES), 1) == 0, 1.0, 0.0).astype(BF16)
        for h in range(n_heads):
            sl = slice(h * HEAD_DIM, (h + 1) * HEAD_DIM)
            q_aug = jnp.concatenate([q_ref[0, :, sl], eqh_sc[h]], axis=1)
            k_aug = jnp.concatenate([k_ref[0, :, sl], ek], axis=1)
            v_aug = jnp.concatenate([v_ref[0, :, sl], ones_col], axis=1)
            s = jnp.where(causal, _dot_nt(q_aug, k_aug), NEG)
            _flash_update(s, v_aug, m_sc.at[h], acc_sc.at[h])

    @pl.when(kj == qi)
    def _():
        for h in range(n_heads):
            sl = slice(h * HEAD_DIM, (h + 1) * HEAD_DIM)
            acc = acc_sc[h]
            o = acc[:, :HEAD_DIM] * (1.0 / acc[:, HEAD_DIM:HEAD_DIM + 1])
            o_ref[0, :, sl] = (o * g_ref[0, :, sl].astype(F32)).astype(o_ref.dtype)


def _fox_prompt(q, kv_bf, gate, eq, ek, bsz, seq, n_heads):
    a_width = n_heads * HEAD_DIM
    t = 512
    nblk = seq // t
    q_blk = lambda b, i, j: (b, i, 0)
    out = pl.pallas_call(
        functools.partial(_fox_prompt_kernel, t=t, n_heads=n_heads),
        grid=(bsz, nblk, nblk),
        in_specs=[pl.BlockSpec((1, t, a_width), q_blk),
                  pl.BlockSpec((1, t, LANES), q_blk),
                  pl.BlockSpec((1, t, a_width), lambda b, i, j: (b, jnp.minimum(j, i), 0)),
                  pl.BlockSpec((1, t, a_width), lambda b, i, j: (b, jnp.minimum(j, i), 1)),
                  pl.BlockSpec((1, t, LANES), lambda b, i, j: (b, jnp.minimum(j, i), 0)),
                  pl.BlockSpec((1, t, a_width), q_blk)],
        out_specs=pl.BlockSpec((1, t, a_width), q_blk),
        out_shape=jax.ShapeDtypeStruct((bsz, seq, a_width), BF16),
        scratch_shapes=[pltpu.VMEM((n_heads, t, LANES), BF16),
                        pltpu.VMEM((n_heads, t, LANES), F32),
                        pltpu.VMEM((n_heads, t, 2 * HEAD_DIM), F32)],
        compiler_params=_params("parallel", "parallel", "arbitrary"),
        name="fox_prompt",
    )(q.reshape(bsz, seq, a_width), eq, kv_bf.reshape(bsz, seq, 2 * a_width),
      kv_bf.reshape(bsz, seq, 2 * a_width), ek, gate.reshape(bsz, seq, a_width))
    return out.reshape(bsz * seq, a_width)


def _fox_sample_kernel(pt_ref, q_ref, lf_ref, kn_ref, vn_ref, g_ref, ka_ref, va_ref, kb_ref, vb_ref,
                       lfa_ref, lfb_ref, o_ref, m_sc, l_sc, acc_sc, carry_sc, cq_sc, *, n_heads, n_tok):
    del pt_ref
    s_idx = pl.program_id(1)
    rows = n_heads * n_tok
    assert rows == LANES and PAGE_SIZE * n_heads == 16 * LANES
    row = _iota((rows, LANES), 0)
    lane = _iota((rows, LANES), 1)
    h_r, t_r = _div_mod_pow2(row, n_tok)
    g_l, h_l = _div_mod_pow2(lane, n_heads)
    same_head = h_l == h_r
    q = q_ref[0]

    @pl.when(s_idx == 0)
    def _():
        lf = lf_ref[0]
        cnew = jnp.sum(jnp.where(same_head & (g_l <= t_r), lf, 0.0), axis=-1, keepdims=True)
        bias = jnp.zeros((rows, LANES), F32)
        for s in range(n_tok):
            c_s = jnp.sum(jnp.where(same_head & (g_l <= s), lf, 0.0), axis=-1, keepdims=True)
            bias = jnp.where(g_l == s, cnew - c_s, bias)
        valid = same_head & (g_l <= t_r)
        sc = jnp.where(valid, _dot_nt(q, kn_ref[0].astype(BF16)) + bias * LOG2E, NEG)
        m = jnp.max(sc, axis=-1, keepdims=True)
        p = jnp.exp2(sc - m)
        m_sc[...] = m
        l_sc[...] = jnp.sum(p, axis=-1, keepdims=True)
        acc_sc[...] = _dot(p.astype(BF16), vn_ref[0].astype(BF16))
        carry_sc[...] = jnp.zeros_like(carry_sc)
        cq_sc[...] = cnew

    @pl.when(s_idx > 0)
    def _():
        src = _iota((LANES, LANES), 0)
        dst = _iota((LANES, LANES), 1)
        s_g, s_h = _div_mod_pow2(src, n_heads)
        d_g, d_h = _div_mod_pow2(dst, n_heads)
        later_in_row = jnp.where((s_h == d_h) & (s_g > d_g), 1.0, 0.0).astype(BF16)
        whole_row = jnp.where(s_h == d_h, 1.0, 0.0).astype(BF16)
        n_kv_rows = PAGE_SIZE * n_heads
        n_tiles = n_kv_rows // LANES
        row_term = jnp.where(same_head, cq_sc[...] * LOG2E, NEG)

        def page_bias(lf16_ref, carry):
            parts = _split3(lf16_ref[...])
            in_row = _dot3(parts, later_in_row)
            row_tot = _dot3(parts, whole_row)
            r_idx = _iota(row_tot.shape, 0)
            after = jnp.zeros_like(row_tot)
            for r2 in range(1, row_tot.shape[0]):
                after = after + jnp.where(r_idx < r2, row_tot[r2:r2 + 1, :], 0.0)
            return (in_row + after + carry) * LOG2E, carry + jnp.sum(row_tot, axis=0, keepdims=True)

        def page_tiles(k_ref, bias16):
            k2 = k_ref[...].reshape(n_kv_rows, HEAD_DIM).astype(BF16)
            s = _dot_nt(q, k2)
            return [s[:, j * LANES:(j + 1) * LANES] + bias16[j:j + 1, :] + row_term for j in range(n_tiles)]

        bias_a, carry = page_bias(lfa_ref, carry_sc[...])
        bias_b, carry = page_bias(lfb_ref, carry)
        carry_sc[...] = carry
        tiles = page_tiles(ka_ref, bias_a) + page_tiles(kb_ref, bias_b)
        tile_max = functools.reduce(jnp.maximum, tiles)
        m_old = m_sc[...]
        m_new = jnp.maximum(m_old, jnp.max(tile_max, axis=-1, keepdims=True))
        alpha = jnp.exp2(m_old - m_new)
        ps = [jnp.exp2(tile - m_new) for tile in tiles]
        p_sum = functools.reduce(lambda a, b: a + b, ps)
        m_sc[...] = m_new
        l_sc[...] = alpha * l_sc[...] + jnp.sum(p_sum, axis=-1, keepdims=True)
        pv = jnp.zeros(acc_sc.shape, F32)
        for v_ref, page_ps in ((va_ref, ps[:n_tiles]), (vb_ref, ps[n_tiles:])):
            p = jnp.concatenate([x.astype(BF16) for x in page_ps], axis=1)
            pv = pv + _dot(p, v_ref[...].reshape(n_kv_rows, HEAD_DIM).astype(BF16))
        acc_sc[...] = alpha * acc_sc[...] + pv

    @pl.when(s_idx == pl.num_programs(1) - 1)
    def _():
        o = acc_sc[...] * (1.0 / l_sc[...])
        for h in range(n_heads):
            sl = slice(h * HEAD_DIM, (h + 1) * HEAD_DIM)
            o_ref[0, :, sl] = o[h * n_tok:(h + 1) * n_tok, :] * g_ref[0, :, sl].astype(F32)


def _fox_sample(q_bf, kv_new, logf_new, gate, cache_kv, cache_lf16, page_table, layer, n_heads):
    bs, n_pages = page_table.shape
    n_tok = q_bf.shape[0] // bs
    a_width = n_heads * HEAD_DIM
    rows = n_heads * n_tok
    qr = q_bf.reshape(bs, n_tok, n_heads, HEAD_DIM).transpose(0, 2, 1, 3).reshape(bs, rows, HEAD_DIM)
    lf = logf_new.reshape(bs, 1, n_tok * n_heads)
    kv5 = kv_new.reshape(bs, n_tok, 2, n_heads, HEAD_DIM)
    k_new = kv5[:, :, 0].reshape(bs, rows, HEAD_DIM)
    v_new = kv5[:, :, 1].reshape(bs, rows, HEAD_DIM)

    assert n_pages % 2 == 0

    def page(b, s, pt, second):
        return pt[b, n_pages - 2 * jnp.maximum(s, 1) + 1 - second]

    def kv_spec(kv, second):
        return pl.BlockSpec((None, None, PAGE_SIZE, None, n_heads, HEAD_DIM),
                            lambda b, s, pt: (layer, page(b, s, pt, second), 0, kv, 0, 0))

    def lf16_spec(second):
        return pl.BlockSpec((None, None, 16, LANES), lambda b, s, pt: (layer, page(b, s, pt, second), 0, 0))

    per_seq = lambda b, s, pt: (b, 0, 0)
    out = pl.pallas_call(
        functools.partial(_fox_sample_kernel, n_heads=n_heads, n_tok=n_tok),
        grid_spec=pltpu.PrefetchScalarGridSpec(
            num_scalar_prefetch=1,
            grid=(bs, n_pages // 2 + 1),
            in_specs=[pl.BlockSpec((1, rows, HEAD_DIM), per_seq),
                      pl.BlockSpec((1, 1, n_tok * n_heads), per_seq),
                      pl.BlockSpec((1, rows, HEAD_DIM), per_seq),
                      pl.BlockSpec((1, rows, HEAD_DIM), per_seq),
                      pl.BlockSpec((1, n_tok, a_width), per_seq),
                      kv_spec(0, 0), kv_spec(1, 0), kv_spec(0, 1), kv_spec(1, 1),
                      lf16_spec(0), lf16_spec(1)],
            out_specs=pl.BlockSpec((1, n_tok, a_width), per_seq),
            scratch_shapes=[pltpu.VMEM((rows, 1), F32),
                            pltpu.VMEM((rows, 1), F32),
                            pltpu.VMEM((rows, HEAD_DIM), F32),
                            pltpu.VMEM((1, LANES), F32),
                            pltpu.VMEM((rows, 1), F32)]),
        out_shape=jax.ShapeDtypeStruct((bs, n_tok, a_width), F32),
        compiler_params=_params("parallel", "arbitrary"),
        name="fox_sample",
    )(page_table, qr, lf, k_new, v_new, gate.reshape(bs, n_tok, a_width),
      cache_kv, cache_kv, cache_kv, cache_kv, cache_lf16, cache_lf16)
    return out.reshape(bs * n_tok, a_width)


def _alibi_slopes(group, n_heads):
    n = N_GROUPS * n_heads
    return tuple(2.0 ** (-8.0 * (group * n_heads + h + 1) / n) for h in range(n_heads))


def _dil_steps(group, n_q_tiles):
    counts = [min(-(-B_WINDOWS[g] // DIL_TILE) + 1, n_q_tiles) for g in range(N_GROUPS)]
    return sum(counts[:group]), counts[group]


def _dil_key_tile(group, n_q_tiles, qi, step):
    first, count = _dil_steps(group, n_q_tiles)
    return qi - jnp.clip(step - first, 0, count - 1)


def _dil_prompt_kernel(q0_ref, q1_ref, q2_ref, k0_ref, v0_ref, k1_ref, v1_ref, k2_ref, v2_ref, g_ref, o_ref,
                       m_sc, acc_sc, *, t, n_heads, n_q_tiles):
    qi = pl.program_id(1)
    step = pl.program_id(2)
    q_refs = (q0_ref, q1_ref, q2_ref)
    k_refs = (k0_ref, k1_ref, k2_ref)
    v_refs = (v0_ref, v1_ref, v2_ref)

    @pl.when(step == 0)
    def _():
        m_sc[...] = jnp.full(m_sc.shape, NEG, F32)
        acc_sc[...] = jnp.zeros(acc_sc.shape, F32)

    for group in range(N_GROUPS):
        first, count = _dil_steps(group, n_q_tiles)
        kt = qi - (step - first)
        window, dil = B_WINDOWS[group], B_DILATIONS[group]

        @pl.when((step >= first) & (step < first + count) & (kt >= 0))
        def _(group=group, kt=kt, window=window, dil=dil):
            delta = (qi - kt) * t + _iota((t, t), 0) - _iota((t, t), 1)
            valid = (delta >= 0) & (delta <= window) & ((delta & (dil - 1)) == 0)
            delta_f = delta.astype(F32)
            ones_col = jnp.where(_iota((t, LANES), 1) == 0, 1.0, 0.0).astype(BF16)
            for h, slope in enumerate(_alibi_slopes(group, n_heads)):
                sl = slice(h * HEAD_DIM, (h + 1) * HEAD_DIM)
                bias = jnp.where(valid, (-slope * LOG2E) * delta_f, NEG)
                s = _dot_nt(q_refs[group][0, :, sl], k_refs[group][0, :, sl]) + bias
                v_aug = jnp.concatenate([v_refs[group][0, :, sl], ones_col], axis=1)
                _flash_update(s, v_aug, m_sc.at[h], acc_sc.at[h])

    @pl.when(step == pl.num_programs(2) - 1)
    def _():
        for h in range(n_heads):
            sl = slice(h * HEAD_DIM, (h + 1) * HEAD_DIM)
            acc = acc_sc[h]
            o = acc[:, :HEAD_DIM] * (1.0 / acc[:, HEAD_DIM:HEAD_DIM + 1])
            o_ref[0, :, sl] = (o * g_ref[0, :, sl].astype(F32)).astype(o_ref.dtype)


def _dil_prompt(q_bf, kv_bfs, gate, bsz, seq, n_heads):
    width = n_heads * HEAD_DIM
    t = DIL_TILE
    assert seq % t == 0 and all(d & (d - 1) == 0 for d in B_DILATIONS)
    n_q_tiles = seq // t
    n_steps = sum(_dil_steps(g, n_q_tiles)[1] for g in range(N_GROUPS))
    blk = (1, t, width)
    q_tile = lambda b, i, s: (b, i, 0)

    def q_spec(group):
        return pl.BlockSpec(blk, lambda b, i, s: (b, i, group))

    def kv_spec(group, col):
        def idx(b, i, s):
            return (b, jnp.maximum(_dil_key_tile(group, n_q_tiles, i, s), 0), col)
        return pl.BlockSpec(blk, idx)

    q3 = q_bf.reshape(bsz, seq, N_GROUPS * width)
    kvs = [kv.reshape(bsz, seq, 2 * width) for kv in kv_bfs]
    out = pl.pallas_call(
        functools.partial(_dil_prompt_kernel, t=t, n_heads=n_heads, n_q_tiles=n_q_tiles),
        grid=(bsz, n_q_tiles, n_steps),
        in_specs=[q_spec(0), q_spec(1), q_spec(2),
                  kv_spec(0, 0), kv_spec(0, 1), kv_spec(1, 0), kv_spec(1, 1), kv_spec(2, 0), kv_spec(2, 1),
                  pl.BlockSpec(blk, q_tile)],
        out_specs=pl.BlockSpec(blk, q_tile),
        out_shape=jax.ShapeDtypeStruct((bsz, seq, width), BF16),
        scratch_shapes=[pltpu.VMEM((n_heads, t, LANES), F32),
                        pltpu.VMEM((n_heads, t, 2 * HEAD_DIM), F32)],
        compiler_params=_params("parallel", "parallel", "arbitrary"),
        name="dil_prompt",
    )(q3, q3, q3, kvs[0], kvs[0], kvs[1], kvs[1], kvs[2], kvs[2], gate.reshape(bsz, seq, width))
    return out.reshape(bsz * seq, width)


def _block_diag_queries(qr, qbd_sc, n_heads, n_tok):
    rows = qr.shape[0]
    head_of_row, _ = _div_mod_pow2(_iota((rows, HEAD_DIM), 0), n_tok)
    for h in range(n_heads):
        qbd_sc[:, h * HEAD_DIM:(h + 1) * HEAD_DIM] = jnp.where(head_of_row == h, qr, jnp.zeros_like(qr))


def _softmax_first(s, v, m_sc, l_sc, acc_sc):
    m = jnp.max(s, axis=-1, keepdims=True)
    p = jnp.exp2(s - m)
    m_sc[...] = m
    l_sc[...] = jnp.sum(p, axis=-1, keepdims=True)
    acc_sc[...] = _dot(p.astype(BF16), v)


def _softmax_next(s, v, m_sc, l_sc, acc_sc):
    m_old = m_sc[...]
    m_new = jnp.maximum(m_old, jnp.max(s, axis=-1, keepdims=True))
    alpha = jnp.exp2(m_old - m_new)
    p = jnp.exp2(s - m_new)
    m_sc[...] = m_new
    l_sc[...] = alpha * l_sc[...] + jnp.sum(p, axis=-1, keepdims=True)
    acc_sc[...] = alpha * acc_sc[...] + _dot(p.astype(BF16), v)


def _dil_sample_kernel(qr_ref, new_ref, buf_ref, o_ref, lse_ref, qbd_sc, m_sc, l_sc, acc_sc,
                       *, window, dil, chunk, slopes, n_tok):
    s_idx = pl.program_id(1)
    n_heads = len(slopes)
    width = n_heads * HEAD_DIM
    rows = n_heads * n_tok

    def geometry(lanes):
        h_r, t_r = _div_mod_pow2(_iota((rows, lanes), 0), n_tok)
        slope = jnp.zeros((rows, lanes), F32)
        for h, sv in enumerate(slopes):
            slope = jnp.where(h_r == h, sv * LOG2E, slope)
        return t_r, _iota((rows, lanes), 1), slope

    def scores(kv, delta, valid, slope):
        k = kv[:, :width].astype(BF16)
        v = kv[:, width:].astype(BF16)
        s = _dot_nt(qbd_sc[...], k) - slope * delta.astype(F32)
        return jnp.where(valid, s, NEG), v

    @pl.when(s_idx == 0)
    def _():
        _block_diag_queries(qr_ref[0], qbd_sc, n_heads, n_tok)
        t_r, lane, slope = geometry(new_ref.shape[1])
        delta = t_r - lane
        valid = (delta >= 0) & ((delta & (dil - 1)) == 0)
        s, v = scores(new_ref[0], delta, valid, slope)
        _softmax_first(s, v, m_sc, l_sc, acc_sc)

    @pl.when(s_idx > 0)
    def _():
        t_r, lane, slope = geometry(chunk)
        delta = window + t_r - ((s_idx - 1) * chunk + lane)
        valid = ((delta & (dil - 1)) == 0) & (delta <= window)
        s, v = scores(buf_ref[0], delta, valid, slope)
        _softmax_next(s, v, m_sc, l_sc, acc_sc)

    @pl.when(s_idx == pl.num_programs(1) - 1)
    def _():
        l = l_sc[...]
        inv_l = 1.0 / l
        for h in range(n_heads):
            sl = slice(h * HEAD_DIM, (h + 1) * HEAD_DIM)
            rs = slice(h * n_tok, (h + 1) * n_tok)
            o_ref[0, :, sl] = acc_sc[rs, sl] * inv_l[rs]
        lse_ref[0] = m_sc[...] * LN2 + jnp.log(l)


def _dil_sample(q_bf, kv_new, buf, group, bs, n_tok, n_heads):
    window, dil = B_WINDOWS[group], B_DILATIONS[group]
    assert dil & (dil - 1) == 0
    width = n_heads * HEAD_DIM
    rows = n_heads * n_tok
    buf_len = buf.shape[1]
    assert buf_len == window, "the sample path assumes a full window buffer"
    chunk = min(buf_len, 512)
    new_rows = 128
    qg = q_bf[:, group * width:(group + 1) * width]
    qr = qg.reshape(bs, n_tok, n_heads, HEAD_DIM).transpose(0, 2, 1, 3).reshape(bs, rows, HEAD_DIM)
    new_pad = jnp.pad(kv_new.reshape(bs, n_tok, 2 * width), ((0, 0), (0, new_rows - n_tok), (0, 0)))
    per_seq = lambda b, s: (b, 0, 0)
    o, lse = pl.pallas_call(
        functools.partial(_dil_sample_kernel, window=window, dil=dil, chunk=chunk,
                          slopes=_alibi_slopes(group, n_heads), n_tok=n_tok),
        grid=(bs, buf_len // chunk + 1),
        in_specs=[pl.BlockSpec((1, rows, HEAD_DIM), per_seq),
                  pl.BlockSpec((1, new_rows, 2 * width), per_seq),
                  pl.BlockSpec((1, chunk, 2 * width), lambda b, s: (b, jnp.maximum(s - 1, 0), 0))],
        out_specs=[pl.BlockSpec((1, n_tok, width), per_seq),
                   pl.BlockSpec((1, rows, 1), per_seq)],
        out_shape=[jax.ShapeDtypeStruct((bs, n_tok, width), F32),
                   jax.ShapeDtypeStruct((bs, rows, 1), F32)],
        scratch_shapes=[pltpu.VMEM((rows, width), BF16),
                        pltpu.VMEM((rows, 1), F32),
                        pltpu.VMEM((rows, 1), F32),
                        pltpu.VMEM((rows, width), F32)],
        compiler_params=_params("parallel", "arbitrary"),
        name=f"dil_sample_g{group}",
    )(qr, new_pad, buf.reshape(bs, buf_len, 2 * width))
    lse = lse.reshape(bs, n_heads, n_tok).transpose(0, 2, 1).reshape(bs * n_tok, n_heads)
    return o.reshape(bs * n_tok, width), lse


def _merge_kernel(o0_ref, o1_ref, o2_ref, l0_ref, l1_ref, l2_ref, g_ref, a_ref, *, n_heads):
    l0, l1, l2 = l0_ref[...], l1_ref[...], l2_ref[...]
    mx = jnp.maximum(jnp.maximum(l0, l1), l2)
    e0, e1, e2 = jnp.exp(l0 - mx), jnp.exp(l1 - mx), jnp.exp(l2 - mx)
    inv = 1.0 / (e0 + e1 + e2)
    w0, w1, w2 = e0 * inv, e1 * inv, e2 * inv
    for h in range(n_heads):
        sl = slice(h * HEAD_DIM, (h + 1) * HEAD_DIM)
        hs = slice(h, h + 1)
        y = w0[:, hs] * o0_ref[:, sl] + w1[:, hs] * o1_ref[:, sl] + w2[:, hs] * o2_ref[:, sl]
        a_ref[:, sl] = (y * g_ref[:, sl].astype(F32)).astype(a_ref.dtype)


def _merge(outs, lses, gate, n_heads):
    m, width = gate.shape
    tm = min(m, 512)
    big = pl.BlockSpec((tm, width), lambda i: (i, 0))
    small = pl.BlockSpec((tm, n_heads), lambda i: (i, 0))
    return pl.pallas_call(
        functools.partial(_merge_kernel, n_heads=n_heads),
        grid=(m // tm,),
        in_specs=[big, big, big, small, small, small, big],
        out_specs=big,
        out_shape=jax.ShapeDtypeStruct((m, width), BF16),
        compiler_params=_params("parallel"),
        name="merge_groups",
    )(*outs, *lses, gate)


def kernel(x_prompt, x_sample, cache_a_kv, cache_a_logf, cache_b0_kv, cache_b1_kv, cache_b2_kv, page_table,
           w_in_a, b_f_a, w_o_a, w_kv_b, w_in_b, w_o_b, ln_g, ln_b):
    bsz, seq, d_model = x_prompt.shape
    bs, n_tok, _ = x_sample.shape
    n_a, n_b = w_in_a.shape[0], w_in_b.shape[0]
    depth = n_a + n_b
    h_a = b_f_a.shape[1]
    a_width = h_a * HEAD_DIM
    h_b = cache_b0_kv.shape[3]
    b_width = h_b * HEAD_DIM
    qk_width = N_GROUPS * b_width
    alpha = (2.0 * depth) ** 0.25
    scale = HEAD_DIM ** -0.5
    b_bufs = (cache_b0_kv, cache_b1_kv, cache_b2_kv)

    w_a = w_in_a
    w_f = jnp.pad(w_in_a[:, :, 4 * a_width:], ((0, 0), (0, 0), (0, LANES - h_a)))
    bf_pad = jnp.pad(b_f_a, ((0, 0), (0, LANES - h_a)))
    w_oa = w_o_a.astype(BF16)
    w_kv = w_kv_b[None]
    w_b = w_in_b
    w_ob = w_o_b.astype(BF16)
    cache_lf16 = cache_a_logf.reshape(n_a, cache_a_logf.shape[1], 16, LANES)

    xp = x_prompt.reshape(bsz * seq, d_model)
    xs = x_sample.reshape(bs * n_tok, d_model)
    xp_in, xs_in = xp.astype(BF16), xs.astype(BF16)
    kv_a_p, logf_a_p, kv_a_s, logf_a_s = [], [], [], []

    for layer in range(n_a):
        ln = (ln_g[layer][None], ln_b[layer][None])
        q = _mm(xp_in, w_a, layer, 0, a_width, "scaled_bf16", scale * LOG2E)
        kv, kv_bf = _mm(xp_in, w_a, layer, a_width, 2 * a_width, "f32_and_bf16")
        gate = _mm(xp_in, w_a, layer, 3 * a_width, a_width, "silu_bf16")
        fl = _mm(xp_in, w_f, layer, 0, LANES, "f32")
        logf, eq, ek = _logf_cumsum(fl, bf_pad[layer][None], bsz, seq, h_a)
        og = _fox_prompt(q, kv_bf, gate, eq, ek, bsz, seq, h_a)
        xp, xp_in = _out_ln(og, w_oa, layer, xp, *ln, alpha)
        kv_a_p.append(kv.reshape(bsz, seq, 2, h_a, HEAD_DIM))
        logf_a_p.append(logf)
        q = _mm(xs_in, w_a, layer, 0, a_width, "scaled_bf16", scale * LOG2E)
        kv, _ = _mm(xs_in, w_a, layer, a_width, 2 * a_width, "f32_and_bf16")
        gate = _mm(xs_in, w_a, layer, 3 * a_width, a_width, "silu_bf16")
        fl = _mm(xs_in, w_f, layer, 0, LANES, "f32")
        logf = _logf(fl, bf_pad[layer][None], h_a)
        og = _fox_sample(q, kv, logf, gate, cache_a_kv, cache_lf16, page_table, layer, h_a)
        xs, xs_in = _out_ln(og, w_oa, layer, xs, *ln, alpha)
        kv_a_s.append(kv.reshape(bs, n_tok, 2, h_a, HEAD_DIM))
        logf_a_s.append(logf.reshape(bs, n_tok, h_a))

    assert b_width == MM_TILE
    kvb_p = [_mm(xp_in, w_kv, 0, g * b_width, 2 * b_width, "f32_and_bf16", col_stride=N_GROUPS)
             for g in range(N_GROUPS)]
    kvb_s = [_mm(xs_in, w_kv, 0, g * b_width, 2 * b_width, "f32_and_bf16", col_stride=N_GROUPS)[0]
             for g in range(N_GROUPS)]

    for j in range(n_b):
        layer = n_a + j
        ln = (ln_g[layer][None], ln_b[layer][None])
        q = _mm(xp_in, w_b, j, 0, qk_width, "scaled_bf16", scale * LOG2E)
        gate = _mm(xp_in, w_b, j, qk_width, b_width, "silu_bf16")
        a = _dil_prompt(q, [kvb_p[g][1] for g in range(N_GROUPS)], gate, bsz, seq, h_b)
        xp, xp_in = _out_ln(a, w_ob, j, xp, *ln, alpha)
        q = _mm(xs_in, w_b, j, 0, qk_width, "scaled_bf16", scale * LOG2E)
        gate = _mm(xs_in, w_b, j, qk_width, b_width, "silu_bf16")
        res = [_dil_sample(q, kvb_s[g], b_bufs[g], g, bs, n_tok, h_b) for g in range(N_GROUPS)]
        a = _merge([r[0] for r in res], [r[1] for r in res], gate, h_b)
        xs, xs_in = _out_ln(a, w_ob, j, xs, *ln, alpha)

    new_b_p, new_b_s = [], []
    for g in range(N_GROUPS):
        kv_p = kvb_p[g][0].reshape(bsz, seq, 2, h_b, HEAD_DIM)
        new_b_p.append(kv_p[:, seq - min(B_WINDOWS[g], seq):])
        cat = jnp.concatenate([b_bufs[g], kvb_s[g].reshape(bs, n_tok, 2, h_b, HEAD_DIM)], axis=1)
        new_b_s.append(cat[:, cat.shape[1] - min(B_WINDOWS[g], cat.shape[1]):])

    return (xp.reshape(bsz, seq, d_model), xs.reshape(bs, n_tok, d_model),
            jnp.stack(kv_a_p), jnp.stack(logf_a_p), jnp.stack(kv_a_s), jnp.stack(logf_a_s),
            *new_b_p, *new_b_s)
```

```python
import functools

import jax
import jax.numpy as jnp
from jax import lax
from jax.experimental import pallas as pl
from jax.experimental.pallas import tpu as pltpu

F32 = jnp.float32
BF16 = jnp.bfloat16

HEAD_DIM = 128
PAGE_SIZE = 128
LANES = 128
B_WINDOWS = (128, 512, 2048)
B_DILATIONS = (1, 4, 16)
N_GROUPS = 3
DIL_TILE = 512
MM_TILE = 1024
FOX_PAGES_PER_STEP = 4
LN_EPS = 1e-5
LOG2E = 1.4426950408889634
LN2 = 0.6931471805599453
NEG = -1e30
V7X_VMEM_LIMIT_BYTES = 56 * 1024 * 1024


def _params(*sem):
    return pltpu.CompilerParams(dimension_semantics=sem, vmem_limit_bytes=V7X_VMEM_LIMIT_BYTES)


def _split3(x):
    hi = x.astype(BF16)
    r1 = x - hi.astype(F32)
    mid = r1.astype(BF16)
    lo = (r1 - mid.astype(F32)).astype(BF16)
    return hi, mid, lo


def _dot(a, b):
    return jnp.dot(a, b, preferred_element_type=F32)


def _dot_nt(a, b):
    return lax.dot_general(a, b, (((1,), (1,)), ((), ())), preferred_element_type=F32)


def _dot3(parts, w):
    hi, mid, lo = parts
    return (_dot(hi, w) + _dot(mid, w)) + _dot(lo, w)


def _dot3_left(w, parts):
    hi, mid, lo = parts
    return (_dot(w, hi) + _dot(w, mid)) + _dot(w, lo)


def _div_mod_pow2(x, n):
    assert n & (n - 1) == 0
    return x >> (n.bit_length() - 1), x & (n - 1)


def _iota(shape, dim):
    return lax.broadcasted_iota(jnp.int32, shape, dim)


def _silu(x):
    return x * (1.0 / (1.0 + jnp.exp(-x)))


def _mm_kernel(x_ref, w_ref, *refs, mode, scale):
    *o_refs, w_sc = refs

    @pl.when(pl.program_id(1) == 0)
    def _():
        w_sc[...] = w_ref[...].astype(BF16)

    acc = _dot(x_ref[...], w_sc[...])
    if mode == "scaled_bf16":
        o_refs[0][...] = (acc * scale).astype(BF16)
    elif mode == "f32_and_bf16":
        o_refs[0][...] = acc
        o_refs[1][...] = acc.astype(BF16)
    elif mode == "silu_bf16":
        o_refs[0][...] = _silu(acc).astype(BF16)
    else:
        o_refs[0][...] = acc


def _mm(x, w, layer, col0, n, mode, scale=1.0, col_stride=1):
    m, k = x.shape
    tn = min(n, MM_TILE)
    tm = min(m, MM_TILE)
    assert m % tm == 0 and n % tn == 0 and col0 % tn == 0
    cb = col0 // tn
    out_shape = [jax.ShapeDtypeStruct((m, n), F32 if mode in ("f32", "f32_and_bf16") else BF16)]
    if mode == "f32_and_bf16":
        out_shape.append(jax.ShapeDtypeStruct((m, n), BF16))
    out_specs = [pl.BlockSpec((tm, tn), lambda j, i: (i, j)) for _ in out_shape]
    res = pl.pallas_call(
        functools.partial(_mm_kernel, mode=mode, scale=scale),
        grid=(n // tn, m // tm),
        in_specs=[pl.BlockSpec((tm, k), lambda j, i: (i, 0)),
                  pl.BlockSpec((None, k, tn), lambda j, i: (layer, 0, cb + j * col_stride))],
        out_specs=out_specs,
        out_shape=out_shape,
        scratch_shapes=[pltpu.VMEM((k, tn), BF16)],
        compiler_params=_params("parallel", "arbitrary"),
        name="proj_" + mode,
    )(x, w)
    return res if mode == "f32_and_bf16" else res[0]


def _out_ln_kernel(a_ref, w_ref, x_ref, g_ref, b_ref, y_ref, ybf_ref, *, alpha):
    mix = _dot(a_ref[...].astype(BF16), w_ref[...])
    z = alpha * x_ref[...] + mix
    mu = jnp.mean(z, axis=-1, keepdims=True)
    zc = z - mu
    var = jnp.mean(zc * zc, axis=-1, keepdims=True)
    y = zc * lax.rsqrt(var + LN_EPS) * g_ref[...] + b_ref[...]
    y_ref[...] = y
    ybf_ref[...] = y.astype(BF16)


def _out_ln(a, w, layer, x, g, b, alpha):
    m, k = a.shape
    d = x.shape[1]
    tm = min(m, 512)
    return pl.pallas_call(
        functools.partial(_out_ln_kernel, alpha=alpha),
        grid=(m // tm,),
        in_specs=[pl.BlockSpec((tm, k), lambda i: (i, 0)),
                  pl.BlockSpec((None, k, d), lambda i: (layer, 0, 0)),
                  pl.BlockSpec((tm, d), lambda i: (i, 0)),
                  pl.BlockSpec((1, d), lambda i: (0, 0)),
                  pl.BlockSpec((1, d), lambda i: (0, 0))],
        out_specs=[pl.BlockSpec((tm, d), lambda i: (i, 0)),
                   pl.BlockSpec((tm, d), lambda i: (i, 0))],
        out_shape=[jax.ShapeDtypeStruct((m, d), F32), jax.ShapeDtypeStruct((m, d), BF16)],
        compiler_params=_params("parallel"),
        name="out_ln",
    )(a, w, x, g, b)


def _log_sigmoid(z):
    return jnp.minimum(z, 0.0) - jnp.log1p(jnp.exp(-jnp.abs(z)))


BIAS_LANES_PER_HEAD = 8


def _logf_cumsum_kernel(fl_ref, bf_ref, logf_ref, eq_ref, ek_ref, *, n_heads, chunk):
    seq = fl_ref.shape[1]
    assert n_heads * BIAS_LANES_PER_HEAD == LANES
    logf = _log_sigmoid(fl_ref[0] + bf_ref[...])
    logf_ref[0] = logf[:, :n_heads]
    tri = jnp.where(_iota((chunk, chunk), 0) >= _iota((chunk, chunk), 1), 1.0, 0.0).astype(BF16)
    src_head = _iota((LANES, LANES), 0)
    dst_head, _ = _div_mod_pow2(_iota((LANES, LANES), 1), BIAS_LANES_PER_HEAD)
    expand = jnp.where(src_head == dst_head, 1.0, 0.0).astype(BF16)
    _, j = _div_mod_pow2(_iota((chunk, LANES), 1), BIAS_LANES_PER_HEAD)
    carry = jnp.zeros((1, LANES), F32)
    for i in range(seq // chunk):
        rows = slice(i * chunk, (i + 1) * chunk)
        cum = _dot3_left(tri, _split3(logf[rows])) + carry
        carry = cum[chunk - 1:chunk]
        c2 = _dot3(_split3(cum), expand) * LOG2E
        hi, mid, lo = (p.astype(F32) for p in _split3(c2))
        first = jnp.where(j == 0, hi, jnp.where(j == 1, mid, lo))
        second = jnp.where(j == 3, hi, jnp.where(j == 4, mid, lo))
        eq_ref[0, rows, :] = jnp.where(j < 3, first, jnp.where(j < 6, 1.0, 0.0)).astype(BF16)
        ek_ref[0, rows, :] = jnp.where(j < 3, 1.0, jnp.where(j < 6, -second, 0.0)).astype(BF16)


def _logf_cumsum(fl, bf_pad, bsz, seq, n_heads):
    lanes = fl.shape[-1]
    return pl.pallas_call(
        functools.partial(_logf_cumsum_kernel, n_heads=n_heads, chunk=256),
        grid=(bsz,),
        in_specs=[pl.BlockSpec((1, seq, lanes), lambda b: (b, 0, 0)),
                  pl.BlockSpec((1, lanes), lambda b: (0, 0))],
        out_specs=[pl.BlockSpec((1, seq, n_heads), lambda b: (b, 0, 0)),
                   pl.BlockSpec((1, seq, lanes), lambda b: (b, 0, 0)),
                   pl.BlockSpec((1, seq, lanes), lambda b: (b, 0, 0))],
        out_shape=[jax.ShapeDtypeStruct((bsz, seq, n_heads), F32),
                   jax.ShapeDtypeStruct((bsz, seq, lanes), BF16),
                   jax.ShapeDtypeStruct((bsz, seq, lanes), BF16)],
        compiler_params=_params("parallel"),
        name="logf_cumsum",
    )(fl.reshape(bsz, seq, lanes), bf_pad)


def _logf_kernel(fl_ref, bf_ref, logf_ref, *, n_heads):
    logf_ref[...] = _log_sigmoid(fl_ref[...] + bf_ref[...])[:, :n_heads]


def _logf(fl, bf_pad, n_heads):
    m, lanes = fl.shape
    return pl.pallas_call(
        functools.partial(_logf_kernel, n_heads=n_heads),
        grid=(1,),
        in_specs=[pl.BlockSpec((m, lanes), lambda i: (0, 0)),
                  pl.BlockSpec((1, lanes), lambda i: (0, 0))],
        out_specs=pl.BlockSpec((m, n_heads), lambda i: (0, 0)),
        out_shape=jax.ShapeDtypeStruct((m, n_heads), F32),
        name="logf",
    )(fl, bf_pad)


def _flash_update(s, v_aug, m_ref, acc_ref):
    m_old = m_ref[...]
    m_new = jnp.maximum(m_old, jnp.max(s, axis=-1, keepdims=True))
    alpha = jnp.exp2(m_old - m_new)
    p = jnp.exp2(s - jnp.concatenate([m_new] * (s.shape[1] // LANES), axis=1)).astype(BF16)
    alpha_wide = jnp.concatenate([alpha] * (acc_ref.shape[1] // LANES), axis=1)
    acc_ref[...] = alpha_wide * acc_ref[...] + _dot(p, v_aug)
    m_ref[...] = m_new


def _fox_prompt_kernel(q_ref, eq_ref, k_ref, v_ref, ek_ref, g_ref, o_ref, eqh_sc, m_sc, acc_sc, *, t, n_heads):
    qi = pl.program_id(1)
    kj = pl.program_id(2)

    @pl.when(kj == 0)
    def _():
        m_sc[...] = jnp.full(m_sc.shape, NEG, F32)
        acc_sc[...] = jnp.zeros(acc_sc.shape, F32)
        eq = eq_ref[0].astype(F32)
        lane_head, _ = _div_mod_pow2(_iota(eq.shape, 1), BIAS_LANES_PER_HEAD)
        for h in range(n_heads):
            eqh_sc[h] = jnp.where(lane_head == h, eq, 0.0).astype(BF16)

    @pl.when(kj <= qi)
    def _():
        causal = (_iota((t, t), 0) + qi * t) >= (_iota((t, t), 1) + kj * t)
        ek = ek_ref[0]
        ones_col = jnp.where(_iota((t, LANES), 1) == 0, 1.0, 0.0).astype(BF16)
        for h in range(n_heads):
            sl = slice(h * HEAD_DIM, (h + 1) * HEAD_DIM)
            q_aug = jnp.concatenate([q_ref[0, :, sl], eqh_sc[h]], axis=1)
            k_aug = jnp.concatenate([k_ref[0, :, sl], ek], axis=1)
            v_aug = jnp.concatenate([v_ref[0, :, sl], ones_col], axis=1)
            s = jnp.where(causal, _dot_nt(q_aug, k_aug), NEG)
            _flash_update(s, v_aug, m_sc.at[h], acc_sc.at[h])

    @pl.when(kj == qi)
    def _():
        for h in range(n_heads):
            sl = slice(h * HEAD_DIM, (h + 1) * HEAD_DIM)
            acc = acc_sc[h]
            o = acc[:, :HEAD_DIM] * (1.0 / acc[:, HEAD_DIM:HEAD_DIM + 1])
            o_ref[0, :, sl] = (o * g_ref[0, :, sl].astype(F32)).astype(o_ref.dtype)


def _fox_prompt(q, kv_bf, gate, eq, ek, bsz, seq, n_heads):
    a_width = n_heads * HEAD_DIM
    t = 512
    nblk = seq // t
    q_blk = lambda b, i, j: (b, i, 0)
    out = pl.pallas_call(
        functools.partial(_fox_prompt_kernel, t=t, n_heads=n_heads),
        grid=(bsz, nblk, nblk),
        in_specs=[pl.BlockSpec((1, t, a_width), q_blk),
                  pl.BlockSpec((1, t, LANES), q_blk),
                  pl.BlockSpec((1, t, a_width), lambda b, i, j: (b, jnp.minimum(j, i), 0)),
                  pl.BlockSpec((1, t, a_width), lambda b, i, j: (b, jnp.minimum(j, i), 1)),
                  pl.BlockSpec((1, t, LANES), lambda b, i, j: (b, jnp.minimum(j, i), 0)),
                  pl.BlockSpec((1, t, a_width), q_blk)],
        out_specs=pl.BlockSpec((1, t, a_width), q_blk),
        out_shape=jax.ShapeDtypeStruct((bsz, seq, a_width), BF16),
        scratch_shapes=[pltpu.VMEM((n_heads, t, LANES), BF16),
                        pltpu.VMEM((n_heads, t, LANES), F32),
                        pltpu.VMEM((n_heads, t, 2 * HEAD_DIM), F32)],
        compiler_params=_params("parallel", "parallel", "arbitrary"),
        name="fox_prompt",
    )(q.reshape(bsz, seq, a_width), eq, kv_bf.reshape(bsz, seq, 2 * a_width),
      kv_bf.reshape(bsz, seq, 2 * a_width), ek, gate.reshape(bsz, seq, a_width))
    return out.reshape(bsz * seq, a_width)


def _fox_sample_kernel(pt_ref, q_ref, lf_ref, kn_ref, vn_ref, g_ref, *refs, n_heads, n_tok, pages_per_step):
    del pt_ref
    page_refs = refs[:pages_per_step]
    lf16_refs = refs[pages_per_step:2 * pages_per_step]
    o_ref, m_sc, l_sc, acc_sc, carry_sc, cq_sc = refs[2 * pages_per_step:]
    s_idx = pl.program_id(1)
    rows = n_heads * n_tok
    assert rows == LANES and PAGE_SIZE * n_heads == 16 * LANES
    row = _iota((rows, LANES), 0)
    lane = _iota((rows, LANES), 1)
    h_r, t_r = _div_mod_pow2(row, n_tok)
    g_l, h_l = _div_mod_pow2(lane, n_heads)
    same_head = h_l == h_r
    q = q_ref[0]

    @pl.when(s_idx == 0)
    def _():
        lf = lf_ref[0]
        cnew = jnp.sum(jnp.where(same_head & (g_l <= t_r), lf, 0.0), axis=-1, keepdims=True)
        bias = jnp.zeros((rows, LANES), F32)
        for s in range(n_tok):
            c_s = jnp.sum(jnp.where(same_head & (g_l <= s), lf, 0.0), axis=-1, keepdims=True)
            bias = jnp.where(g_l == s, cnew - c_s, bias)
        valid = same_head & (g_l <= t_r)
        sc = jnp.where(valid, _dot_nt(q, kn_ref[0].astype(BF16)) + bias * LOG2E, NEG)
        m = jnp.max(sc, axis=-1, keepdims=True)
        p = jnp.exp2(sc - m)
        m_sc[...] = m
        l_sc[...] = jnp.sum(p, axis=-1, keepdims=True)
        acc_sc[...] = _dot(p.astype(BF16), vn_ref[0].astype(BF16))
        carry_sc[...] = jnp.zeros_like(carry_sc)
        cq_sc[...] = cnew

    @pl.when(s_idx > 0)
    def _():
        src = _iota((LANES, LANES), 0)
        dst = _iota((LANES, LANES), 1)
        s_g, s_h = _div_mod_pow2(src, n_heads)
        d_g, d_h = _div_mod_pow2(dst, n_heads)
        later_in_row = jnp.where((s_h == d_h) & (s_g > d_g), 1.0, 0.0).astype(BF16)
        whole_row = jnp.where(s_h == d_h, 1.0, 0.0).astype(BF16)
        n_kv_rows = PAGE_SIZE * n_heads
        n_tiles = n_kv_rows // LANES
        row_term = jnp.where(same_head, cq_sc[...] * LOG2E, NEG)

        def page_bias(lf16_ref, carry):
            parts = _split3(lf16_ref[...])
            in_row = _dot3(parts, later_in_row)
            row_tot = _dot3(parts, whole_row)
            r_idx = _iota(row_tot.shape, 0)
            after = jnp.zeros_like(row_tot)
            for r2 in range(1, row_tot.shape[0]):
                after = after + jnp.where(r_idx < r2, row_tot[r2:r2 + 1, :], 0.0)
            return (in_row + after + carry) * LOG2E, carry + jnp.sum(row_tot, axis=0, keepdims=True)

        def page_tiles(page_ref, bias16):
            k2 = page_ref[:, 0].reshape(n_kv_rows, HEAD_DIM).astype(BF16)
            s = _dot_nt(q, k2)
            return [s[:, j * LANES:(j + 1) * LANES] + bias16[j:j + 1, :] + row_term for j in range(n_tiles)]

        carry = carry_sc[...]
        tiles = []
        for page_ref, lf16_ref in zip(page_refs, lf16_refs):
            bias16, carry = page_bias(lf16_ref, carry)
            tiles += page_tiles(page_ref, bias16)
        carry_sc[...] = carry
        tile_max = functools.reduce(jnp.maximum, tiles)
        m_old = m_sc[...]
        m_new = jnp.maximum(m_old, jnp.max(tile_max, axis=-1, keepdims=True))
        alpha = jnp.exp2(m_old - m_new)
        ps = [jnp.exp2(tile - m_new) for tile in tiles]
        p_sum = functools.reduce(lambda a, b: a + b, ps)
        m_sc[...] = m_new
        l_sc[...] = alpha * l_sc[...] + jnp.sum(p_sum, axis=-1, keepdims=True)
        pv = jnp.zeros(acc_sc.shape, F32)
        for i, page_ref in enumerate(page_refs):
            p = jnp.concatenate([x.astype(BF16) for x in ps[i * n_tiles:(i + 1) * n_tiles]], axis=1)
            pv = pv + _dot(p, page_ref[:, 1].reshape(n_kv_rows, HEAD_DIM).astype(BF16))
        acc_sc[...] = alpha * acc_sc[...] + pv

    @pl.when(s_idx == pl.num_programs(1) - 1)
    def _():
        o = acc_sc[...] * (1.0 / l_sc[...])
        for h in range(n_heads):
            sl = slice(h * HEAD_DIM, (h + 1) * HEAD_DIM)
            o_ref[0, :, sl] = o[h * n_tok:(h + 1) * n_tok, :] * g_ref[0, :, sl].astype(F32)


def _fox_sample(q_bf, kv_new, logf_new, gate, cache_kv, cache_lf16, page_table, layer, n_heads):
    bs, n_pages = page_table.shape
    n_tok = q_bf.shape[0] // bs
    a_width = n_heads * HEAD_DIM
    rows = n_heads * n_tok
    qr = q_bf.reshape(bs, n_tok, n_heads, HEAD_DIM).transpose(0, 2, 1, 3).reshape(bs, rows, HEAD_DIM)
    lf = logf_new.reshape(bs, 1, n_tok * n_heads)
    kv5 = kv_new.reshape(bs, n_tok, 2, n_heads, HEAD_DIM)
    k_new = kv5[:, :, 0].reshape(bs, rows, HEAD_DIM)
    v_new = kv5[:, :, 1].reshape(bs, rows, HEAD_DIM)

    npp = FOX_PAGES_PER_STEP
    assert n_pages % npp == 0

    def page(b, s, pt, i):
        return pt[b, n_pages - npp * (jnp.maximum(s, 1) - 1) - 1 - i]

    def page_spec(i):
        return pl.BlockSpec((None, None, PAGE_SIZE, 2, n_heads, HEAD_DIM),
                            lambda b, s, pt: (layer, page(b, s, pt, i), 0, 0, 0, 0))

    def lf16_spec(i):
        return pl.BlockSpec((None, None, 16, LANES), lambda b, s, pt: (layer, page(b, s, pt, i), 0, 0))

    per_seq = lambda b, s, pt: (b, 0, 0)
    out = pl.pallas_call(
        functools.partial(_fox_sample_kernel, n_heads=n_heads, n_tok=n_tok, pages_per_step=npp),
        grid_spec=pltpu.PrefetchScalarGridSpec(
            num_scalar_prefetch=1,
            grid=(bs, n_pages // npp + 1),
            in_specs=[pl.BlockSpec((1, rows, HEAD_DIM), per_seq),
                      pl.BlockSpec((1, 1, n_tok * n_heads), per_seq),
                      pl.BlockSpec((1, rows, HEAD_DIM), per_seq),
                      pl.BlockSpec((1, rows, HEAD_DIM), per_seq),
                      pl.BlockSpec((1, n_tok, a_width), per_seq),
                      *[page_spec(i) for i in range(npp)],
                      *[lf16_spec(i) for i in range(npp)]],
            out_specs=pl.BlockSpec((1, n_tok, a_width), per_seq),
            scratch_shapes=[pltpu.VMEM((rows, 1), F32),
                            pltpu.VMEM((rows, 1), F32),
                            pltpu.VMEM((rows, HEAD_DIM), F32),
                            pltpu.VMEM((1, LANES), F32),
                            pltpu.VMEM((rows, 1), F32)]),
        out_shape=jax.ShapeDtypeStruct((bs, n_tok, a_width), F32),
        compiler_params=_params("parallel", "arbitrary"),
        name="fox_sample",
    )(page_table, qr, lf, k_new, v_new, gate.reshape(bs, n_tok, a_width),
      *([cache_kv] * npp), *([cache_lf16] * npp))
    return out.reshape(bs * n_tok, a_width)


def _alibi_slopes(group, n_heads):
    n = N_GROUPS * n_heads
    return tuple(2.0 ** (-8.0 * (group * n_heads + h + 1) / n) for h in range(n_heads))


def _dil_steps(group, n_q_tiles):
    counts = [min(-(-B_WINDOWS[g] // DIL_TILE) + 1, n_q_tiles) for g in range(N_GROUPS)]
    return sum(counts[:group]), counts[group]


def _dil_key_tile(group, n_q_tiles, qi, step):
    first, count = _dil_steps(group, n_q_tiles)
    return qi - jnp.clip(step - first, 0, count - 1)


def _dil_prompt_kernel(q0_ref, q1_ref, q2_ref, k0_ref, v0_ref, k1_ref, v1_ref, k2_ref, v2_ref, g_ref, o_ref,
                       m_sc, acc_sc, *, t, n_heads, n_q_tiles):
    qi = pl.program_id(1)
    step = pl.program_id(2)
    q_refs = (q0_ref, q1_ref, q2_ref)
    k_refs = (k0_ref, k1_ref, k2_ref)
    v_refs = (v0_ref, v1_ref, v2_ref)

    @pl.when(step == 0)
    def _():
        m_sc[...] = jnp.full(m_sc.shape, NEG, F32)
        acc_sc[...] = jnp.zeros(acc_sc.shape, F32)

    for group in range(N_GROUPS):
        first, count = _dil_steps(group, n_q_tiles)
        kt = qi - (step - first)
        window, dil = B_WINDOWS[group], B_DILATIONS[group]

        @pl.when((step >= first) & (step < first + count) & (kt >= 0))
        def _(group=group, kt=kt, window=window, dil=dil):
            delta = (qi - kt) * t + _iota((t, t), 0) - _iota((t, t), 1)
            valid = (delta >= 0) & (delta <= window) & ((delta & (dil - 1)) == 0)
            delta_f = delta.astype(F32)
            ones_col = jnp.where(_iota((t, LANES), 1) == 0, 1.0, 0.0).astype(BF16)
            for h, slope in enumerate(_alibi_slopes(group, n_heads)):
                sl = slice(h * HEAD_DIM, (h + 1) * HEAD_DIM)
                bias = jnp.where(valid, (-slope * LOG2E) * delta_f, NEG)
                s = _dot_nt(q_refs[group][0, :, sl], k_refs[group][0, :, sl]) + bias
                v_aug = jnp.concatenate([v_refs[group][0, :, sl], ones_col], axis=1)
                _flash_update(s, v_aug, m_sc.at[h], acc_sc.at[h])

    @pl.when(step == pl.num_programs(2) - 1)
    def _():
        for h in range(n_heads):
            sl = slice(h * HEAD_DIM, (h + 1) * HEAD_DIM)
            acc = acc_sc[h]
            o = acc[:, :HEAD_DIM] * (1.0 / acc[:, HEAD_DIM:HEAD_DIM + 1])
            o_ref[0, :, sl] = (o * g_ref[0, :, sl].astype(F32)).astype(o_ref.dtype)


def _dil_prompt(q_bf, kv_bfs, gate, bsz, seq, n_heads):
    width = n_heads * HEAD_DIM
    t = DIL_TILE
    assert seq % t == 0 and all(d & (d - 1) == 0 for d in B_DILATIONS)
    n_q_tiles = seq // t
    n_steps = sum(_dil_steps(g, n_q_tiles)[1] for g in range(N_GROUPS))
    blk = (1, t, width)
    q_tile = lambda b, i, s: (b, i, 0)

    def q_spec(group):
        return pl.BlockSpec(blk, lambda b, i, s: (b, i, group))

    def kv_spec(group, col):
        def idx(b, i, s):
            return (b, jnp.maximum(_dil_key_tile(group, n_q_tiles, i, s), 0), col)
        return pl.BlockSpec(blk, idx)

    q3 = q_bf.reshape(bsz, seq, N_GROUPS * width)
    kvs = [kv.reshape(bsz, seq, 2 * width) for kv in kv_bfs]
    out = pl.pallas_call(
        functools.partial(_dil_prompt_kernel, t=t, n_heads=n_heads, n_q_tiles=n_q_tiles),
        grid=(bsz, n_q_tiles, n_steps),
        in_specs=[q_spec(0), q_spec(1), q_spec(2),
                  kv_spec(0, 0), kv_spec(0, 1), kv_spec(1, 0), kv_spec(1, 1), kv_spec(2, 0), kv_spec(2, 1),
                  pl.BlockSpec(blk, q_tile)],
        out_specs=pl.BlockSpec(blk, q_tile),
        out_shape=jax.ShapeDtypeStruct((bsz, seq, width), BF16),
        scratch_shapes=[pltpu.VMEM((n_heads, t, LANES), F32),
                        pltpu.VMEM((n_heads, t, 2 * HEAD_DIM), F32)],
        compiler_params=_params("parallel", "parallel", "arbitrary"),
        name="dil_prompt",
    )(q3, q3, q3, kvs[0], kvs[0], kvs[1], kvs[1], kvs[2], kvs[2], gate.reshape(bsz, seq, width))
    return out.reshape(bsz * seq, width)


def _dil_sample_kernel(q_ref, kn_ref, vn_ref, buf_ref, o_ref, lse_ref, m_sc, l_sc, acc_sc,
                       *, window, dil, chunk, slopes, n_tok):
    s_idx = pl.program_id(1)
    n_heads = len(slopes)
    rows = n_heads * n_tok
    pos_per_tile = LANES // n_heads
    h_r, t_r = _div_mod_pow2(_iota((rows, LANES), 0), n_tok)
    p_l, h_l = _div_mod_pow2(_iota((rows, LANES), 1), n_heads)
    same_head = h_l == h_r
    slope = jnp.zeros((rows, LANES), F32)
    for h, sv in enumerate(slopes):
        slope = jnp.where(h_r == h, sv * LOG2E, slope)
    back = t_r - p_l
    q = q_ref[0]

    def masked(s_tile, delta):
        valid = same_head & (delta >= 0) & (delta <= window) & ((delta & (dil - 1)) == 0)
        return jnp.where(valid, s_tile - slope * delta.astype(F32), NEG)

    @pl.when(s_idx == 0)
    def _():
        s = masked(_dot_nt(q, kn_ref[0].astype(BF16)), back)
        m = jnp.max(s, axis=-1, keepdims=True)
        p = jnp.exp2(s - m)
        m_sc[...] = m
        l_sc[...] = jnp.sum(p, axis=-1, keepdims=True)
        acc_sc[...] = _dot(p.astype(BF16), vn_ref[0].astype(BF16))

    @pl.when(s_idx > 0)
    def _():
        n_kv_rows = chunk * n_heads
        k2 = buf_ref[0, :, 0].reshape(n_kv_rows, HEAD_DIM).astype(BF16)
        v2 = buf_ref[0, :, 1].reshape(n_kv_rows, HEAD_DIM).astype(BF16)
        s = _dot_nt(q, k2)
        first = window - (s_idx - 1) * chunk
        tiles = [masked(s[:, j * LANES:(j + 1) * LANES], back + (first - j * pos_per_tile))
                 for j in range(n_kv_rows // LANES)]
        tile_max = functools.reduce(jnp.maximum, tiles)
        m_old = m_sc[...]
        m_new = jnp.maximum(m_old, jnp.max(tile_max, axis=-1, keepdims=True))
        alpha = jnp.exp2(m_old - m_new)
        ps = [jnp.exp2(tile - m_new) for tile in tiles]
        p_sum = functools.reduce(lambda a, b: a + b, ps)
        m_sc[...] = m_new
        l_sc[...] = alpha * l_sc[...] + jnp.sum(p_sum, axis=-1, keepdims=True)
        p = jnp.concatenate([x.astype(BF16) for x in ps], axis=1)
        acc_sc[...] = alpha * acc_sc[...] + _dot(p, v2)

    @pl.when(s_idx == pl.num_programs(1) - 1)
    def _():
        l = l_sc[...]
        o = acc_sc[...] * (1.0 / l)
        for h in range(n_heads):
            o_ref[0, :, h * HEAD_DIM:(h + 1) * HEAD_DIM] = o[h * n_tok:(h + 1) * n_tok, :]
        lse_ref[0] = m_sc[...] * LN2 + jnp.log(l)


def _dil_sample(q_bf, kv_new, buf, group, bs, n_tok, n_heads):
    window, dil = B_WINDOWS[group], B_DILATIONS[group]
    assert dil & (dil - 1) == 0
    width = n_heads * HEAD_DIM
    rows = n_heads * n_tok
    buf_len = buf.shape[1]
    assert buf_len == window, "the sample path assumes a full window buffer"
    chunk = min(buf_len, 512)
    pos_per_tile = LANES // n_heads
    assert n_tok <= pos_per_tile
    qg = q_bf[:, group * width:(group + 1) * width]
    qr = qg.reshape(bs, n_tok, n_heads, HEAD_DIM).transpose(0, 2, 1, 3).reshape(bs, rows, HEAD_DIM)
    kv5 = jnp.pad(kv_new.reshape(bs, n_tok, 2, n_heads, HEAD_DIM),
                  ((0, 0), (0, pos_per_tile - n_tok), (0, 0), (0, 0), (0, 0)))
    k_new = kv5[:, :, 0].reshape(bs, LANES, HEAD_DIM)
    v_new = kv5[:, :, 1].reshape(bs, LANES, HEAD_DIM)
    per_seq = lambda b, s: (b, 0, 0)
    o, lse = pl.pallas_call(
        functools.partial(_dil_sample_kernel, window=window, dil=dil, chunk=chunk,
                          slopes=_alibi_slopes(group, n_heads), n_tok=n_tok),
        grid=(bs, buf_len // chunk + 1),
        in_specs=[pl.BlockSpec((1, rows, HEAD_DIM), per_seq),
                  pl.BlockSpec((1, LANES, HEAD_DIM), per_seq),
                  pl.BlockSpec((1, LANES, HEAD_DIM), per_seq),
                  pl.BlockSpec((1, chunk, 2, n_heads, HEAD_DIM),
                               lambda b, s: (b, jnp.maximum(s - 1, 0), 0, 0, 0))],
        out_specs=[pl.BlockSpec((1, n_tok, width), per_seq),
                   pl.BlockSpec((1, rows, 1), per_seq)],
        out_shape=[jax.ShapeDtypeStruct((bs, n_tok, width), F32),
                   jax.ShapeDtypeStruct((bs, rows, 1), F32)],
        scratch_shapes=[pltpu.VMEM((rows, 1), F32),
                        pltpu.VMEM((rows, 1), F32),
                        pltpu.VMEM((rows, HEAD_DIM), F32)],
        compiler_params=_params("parallel", "arbitrary"),
        name=f"dil_sample_g{group}",
    )(qr, k_new, v_new, buf)
    lse = lse.reshape(bs, n_heads, n_tok).transpose(0, 2, 1).reshape(bs * n_tok, n_heads)
    return o.reshape(bs * n_tok, width), lse


def _merge_kernel(o0_ref, o1_ref, o2_ref, l0_ref, l1_ref, l2_ref, g_ref, a_ref, *, n_heads):
    l0, l1, l2 = l0_ref[...], l1_ref[...], l2_ref[...]
    mx = jnp.maximum(jnp.maximum(l0, l1), l2)
    e0, e1, e2 = jnp.exp(l0 - mx), jnp.exp(l1 - mx), jnp.exp(l2 - mx)
    inv = 1.0 / (e0 + e1 + e2)
    w0, w1, w2 = e0 * inv, e1 * inv, e2 * inv
    for h in range(n_heads):
        sl = slice(h * HEAD_DIM, (h + 1) * HEAD_DIM)
        hs = slice(h, h + 1)
        y = w0[:, hs] * o0_ref[:, sl] + w1[:, hs] * o1_ref[:, sl] + w2[:, hs] * o2_ref[:, sl]
        a_ref[:, sl] = (y * g_ref[:, sl].astype(F32)).astype(a_ref.dtype)


def _merge(outs, lses, gate, n_heads):
    m, width = gate.shape
    tm = min(m, 512)
    big = pl.BlockSpec((tm, width), lambda i: (i, 0))
    small = pl.BlockSpec((tm, n_heads), lambda i: (i, 0))
    return pl.pallas_call(
        functools.partial(_merge_kernel, n_heads=n_heads),
        grid=(m // tm,),
        in_specs=[big, big, big, small, small, small, big],
        out_specs=big,
        out_shape=jax.ShapeDtypeStruct((m, width), BF16),
        compiler_params=_params("parallel"),
        name="merge_groups",
    )(*outs, *lses, gate)


def kernel(x_prompt, x_sample, cache_a_kv, cache_a_logf, cache_b0_kv, cache_b1_kv, cache_b2_kv, page_table,
           w_in_a, b_f_a, w_o_a, w_kv_b, w_in_b, w_o_b, ln_g, ln_b):
    bsz, seq, d_model = x_prompt.shape
    bs, n_tok, _ = x_sample.shape
    n_a, n_b = w_in_a.shape[0], w_in_b.shape[0]
    depth = n_a + n_b
    h_a = b_f_a.shape[1]
    a_width = h_a * HEAD_DIM
    h_b = cache_b0_kv.shape[3]
    b_width = h_b * HEAD_DIM
    qk_width = N_GROUPS * b_width
    alpha = (2.0 * depth) ** 0.25
    scale = HEAD_DIM ** -0.5
    b_bufs = (cache_b0_kv, cache_b1_kv, cache_b2_kv)

    w_a = w_in_a[:, :, :4 * a_width].astype(BF16)
    w_f = jnp.pad(w_in_a[:, :, 4 * a_width:], ((0, 0), (0, 0), (0, LANES - h_a)))
    bf_pad = jnp.pad(b_f_a, ((0, 0), (0, LANES - h_a)))
    w_oa = w_o_a.astype(BF16)
    w_kv = w_kv_b[None]
    w_b = w_in_b
    w_ob = w_o_b.astype(BF16)
    cache_lf16 = cache_a_logf.reshape(n_a, cache_a_logf.shape[1], 16, LANES)

    xp = x_prompt.reshape(bsz * seq, d_model)
    xs = x_sample.reshape(bs * n_tok, d_model)
    xp_in, xs_in = xp.astype(BF16), xs.astype(BF16)
    kv_a_p, logf_a_p, kv_a_s, logf_a_s = [], [], [], []

    for layer in range(n_a):
        ln = (ln_g[layer][None], ln_b[layer][None])
        q = _mm(xp_in, w_a, layer, 0, a_width, "scaled_bf16", scale * LOG2E)
        kv, kv_bf = _mm(xp_in, w_a, layer, a_width, 2 * a_width, "f32_and_bf16")
        gate = _mm(xp_in, w_a, layer, 3 * a_width, a_width, "silu_bf16")
        fl = _mm(xp_in, w_f, layer, 0, LANES, "f32")
        logf, eq, ek = _logf_cumsum(fl, bf_pad[layer][None], bsz, seq, h_a)
        og = _fox_prompt(q, kv_bf, gate, eq, ek, bsz, seq, h_a)
        xp, xp_in = _out_ln(og, w_oa, layer, xp, *ln, alpha)
        kv_a_p.append(kv.reshape(bsz, seq, 2, h_a, HEAD_DIM))
        logf_a_p.append(logf)
        q = _mm(xs_in, w_a, layer, 0, a_width, "scaled_bf16", scale * LOG2E)
        kv, _ = _mm(xs_in, w_a, layer, a_width, 2 * a_width, "f32_and_bf16")
        gate = _mm(xs_in, w_a, layer, 3 * a_width, a_width, "silu_bf16")
        fl = _mm(xs_in, w_f, layer, 0, LANES, "f32")
        logf = _logf(fl, bf_pad[layer][None], h_a)
        og = _fox_sample(q, kv, logf, gate, cache_a_kv, cache_lf16, page_table, layer, h_a)
        xs, xs_in = _out_ln(og, w_oa, layer, xs, *ln, alpha)
        kv_a_s.append(kv.reshape(bs, n_tok, 2, h_a, HEAD_DIM))
        logf_a_s.append(logf.reshape(bs, n_tok, h_a))

    assert b_width == MM_TILE
    kvb_p = [_mm(xp_in, w_kv, 0, g * b_width, 2 * b_width, "f32_and_bf16", col_stride=N_GROUPS)
             for g in range(N_GROUPS)]
    kvb_s = [_mm(xs_in, w_kv, 0, g * b_width, 2 * b_width, "f32_and_bf16", col_stride=N_GROUPS)[0]
             for g in range(N_GROUPS)]

    for j in range(n_b):
        layer = n_a + j
        ln = (ln_g[layer][None], ln_b[layer][None])
        q = _mm(xp_in, w_b, j, 0, qk_width, "scaled_bf16", scale * LOG2E)
        gate = _mm(xp_in, w_b, j, qk_width, b_width, "silu_bf16")
        a = _dil_prompt(q, [kvb_p[g][1] for g in range(N_GROUPS)], gate, bsz, seq, h_b)
        xp, xp_in = _out_ln(a, w_ob, j, xp, *ln, alpha)
        q = _mm(xs_in, w_b, j, 0, qk_width, "scaled_bf16", scale * LOG2E)
        gate = _mm(xs_in, w_b, j, qk_width, b_width, "silu_bf16")
        res = [_dil_sample(q, kvb_s[g], b_bufs[g], g, bs, n_tok, h_b) for g in range(N_GROUPS)]
        a = _merge([r[0] for r in res], [r[1] for r in res], gate, h_b)
        xs, xs_in = _out_ln(a, w_ob, j, xs, *ln, alpha)

    new_b_p, new_b_s = [], []
    for g in range(N_GROUPS):
        kv_p = kvb_p[g][0].reshape(bsz, seq, 2, h_b, HEAD_DIM)
        new_b_p.append(kv_p[:, seq - min(B_WINDOWS[g], seq):])
        cat = jnp.concatenate([b_bufs[g], kvb_s[g].reshape(bs, n_tok, 2, h_b, HEAD_DIM)], axis=1)
        new_b_s.append(cat[:, cat.shape[1] - min(B_WINDOWS[g], cat.shape[1]):])

    return (xp.reshape(bsz, seq, d_model), xs.reshape(bs, n_tok, d_model),
            jnp.stack(kv_a_p), jnp.stack(logf_a_p), jnp.stack(kv_a_s), jnp.stack(logf_a_s),
            *new_b_p, *new_b_s)
```
